```python
import math
import jax, jax.numpy as jnp
from jax import lax
import numpy as np

D_MODEL = 1024
BATCH = 4
SEQ = 8192
DEPTH = 2

HGRN_HEADS = 4
HGRN_DK = D_MODEL // (2 * HGRN_HEADS)
HGRN_DV = D_MODEL // (2 * HGRN_HEADS)
RET_HEADS = 4
RET_DK = D_MODEL // (2 * RET_HEADS)
RET_DV = D_MODEL // (2 * RET_HEADS)
LIN_CHUNK = 64
EVEN_COLS = [HGRN_HEADS * HGRN_DK, HGRN_HEADS * HGRN_DK, HGRN_HEADS * HGRN_DV, HGRN_HEADS * HGRN_DV,
             RET_HEADS * RET_DK, RET_HEADS * RET_DK, RET_HEADS * RET_DV, RET_HEADS * RET_DV]
EVEN_MIX_OUT = HGRN_HEADS * HGRN_DV + RET_HEADS * RET_DV

NSA_HD = 64
NSA_HEADS = D_MODEL // NSA_HD
NSA_KV_GROUPS = 2
CMP_BLOCK = 32
CMP_STRIDE = 16
SLC_BLOCK = 64
SLC_TOPN = 16
WINDOW = 512
Q_BLOCK = 128
KVW = NSA_KV_GROUPS * NSA_HD
ODD_COLS = [NSA_HEADS * NSA_HD, KVW, KVW, KVW, KVW, KVW, KVW, 3 * NSA_HEADS]

FFN_HIDDEN = ((8 * D_MODEL // 3 + 255) // 256) * 256
N_EVEN = (DEPTH + 1) // 2
N_ODD = DEPTH // 2
RMS_EPS = 1e-6
NEG_INF = -1e30
FORCE_SCORE = 1e9

kernel_name = "hgrn2_retention_nsa_hybrid"


def rmsnorm(x, g):
    xf = x.astype(jnp.float32)
    y = xf * lax.rsqrt(jnp.mean(xf * xf, axis=-1, keepdims=True) + RMS_EPS)
    return (y * g.astype(jnp.float32)).astype(x.dtype)


def split_cols(a, sizes):
    return jnp.split(a, np.cumsum(sizes)[:-1].tolist(), axis=-1)


def to_chunks(a, c):
    b, t, h, d = a.shape
    return a.reshape(b, t // c, c, h, d).transpose(1, 0, 3, 2, 4)


def from_chunks(a):
    n, b, h, c, d = a.shape
    return a.transpose(1, 0, 3, 2, 4).reshape(b, n * c, h, d)


def hgrn2_chunked(q, k, v, log_f):
    b_, t_, h_, dk = q.shape
    dv = v.shape[-1]
    c = LIN_CHUNK
    causal = jnp.tril(jnp.ones((c, c), dtype=bool))[:, :, None]

    def step(state, inp):
        qi, ki, vi, gi = inp
        cum = jnp.cumsum(gi, axis=2)
        diff = cum[:, :, :, None, :] - cum[:, :, None, :, :]
        decay = jnp.exp(jnp.where(causal, diff, NEG_INF))
        attn = jnp.einsum('bhid,bhjd,bhijd->bhij', qi, ki, decay)
        out = jnp.einsum('bhij,bhjv->bhiv', attn, vi) + jnp.einsum('bhid,bhdv->bhiv', qi * jnp.exp(cum), state)
        last = cum[:, :, -1:, :]
        state = jnp.exp(last[:, :, 0, :])[..., None] * state + jnp.einsum('bhjd,bhjv->bhdv', ki * jnp.exp(last - cum), vi)
        return state, out

    s0 = jnp.zeros((b_, h_, dk, dv), jnp.float32)
    xs = tuple(to_chunks(a.astype(jnp.float32), c) for a in (q, k, v, log_f))
    _, o = lax.scan(step, s0, xs)
    return from_chunks(o)


def retention_chunked(q, k, v, log_gamma):
    b_, t_, h_, dk = q.shape
    dv = v.shape[-1]
    c = LIN_CHUNK
    pos = jnp.arange(c, dtype=jnp.float32)
    rel = pos[:, None] - pos[None, :]
    decay = jnp.where(rel[None] >= 0, jnp.exp(jnp.maximum(rel, 0.0)[None] * log_gamma[:, None, None]), 0.0)
    q_decay = jnp.exp((pos + 1.0)[None, :] * log_gamma[:, None])[..., None]
    k_decay = jnp.exp((c - 1.0 - pos)[None, :] * log_gamma[:, None])[..., None]
    chunk_decay = jnp.exp(c * log_gamma)[:, None, None]

    def step(state, inp):
        qi, ki, vi = inp
        attn = jnp.einsum('bhid,bhjd->bhij', qi, ki) * decay
        out = jnp.einsum('bhij,bhjv->bhiv', attn, vi) + jnp.einsum('bhid,bhdv->bhiv', qi * q_decay, state)
        state = chunk_decay * state + jnp.einsum('bhjd,bhjv->bhdv', ki * k_decay, vi)
        return state, out

    s0 = jnp.zeros((b_, h_, dk, dv), jnp.float32)
    xs = tuple(to_chunks(a.astype(jnp.float32), c) for a in (q, k, v))
    _, o = lax.scan(step, s0, xs)
    return from_chunks(o)


def even_mixer(h, w_in, lower_bound, hgrn_norm, ret_norm, w_out):
    b_, t_, _ = h.shape
    hq, hf, hi, hg, rq, rk, rv, rg = split_cols(h @ w_in, EVEN_COLS)
    heads = lambda a, n: a.reshape(b_, t_, n, -1)
    f = lower_bound + (1.0 - lower_bound) * jax.nn.sigmoid(hf.astype(jnp.float32))
    o_h = hgrn2_chunked(heads(jax.nn.silu(hq), HGRN_HEADS), heads(1.0 - f, HGRN_HEADS),
                        heads(hi, HGRN_HEADS), heads(jnp.log(f), HGRN_HEADS))
    o_h = rmsnorm(o_h, hgrn_norm) * jax.nn.silu(heads(hg, HGRN_HEADS).astype(jnp.float32))
    log_gamma = jnp.log(1.0 - jnp.exp2(-5.0 - jnp.arange(RET_HEADS, dtype=jnp.float32)))
    o_r = retention_chunked(heads(rq, RET_HEADS), heads(rk, RET_HEADS) * (RET_DK ** -0.5),
                            heads(rv, RET_HEADS), log_gamma)
    o_r = rmsnorm(o_r, ret_norm) * jax.nn.silu(heads(rg, RET_HEADS).astype(jnp.float32))
    o = jnp.concatenate([o_h.reshape(b_, t_, -1), o_r.reshape(b_, t_, -1)], axis=-1)
    return o.astype(h.dtype) @ w_out


def alibi_slopes(n):
    return jnp.exp2(-8.0 * jnp.arange(1, n + 1, dtype=jnp.float32) / n)


def compress_blocks(k, pos_emb, w1, w2):
    b_, t_, g_, d = k.shape
    r = CMP_BLOCK // CMP_STRIDE
    ch = k.reshape(b_, t_ // CMP_STRIDE, CMP_STRIDE, g_, d)
    nc = t_ // CMP_STRIDE - r + 1
    blocks = jnp.concatenate([ch[:, j:j + nc] for j in range(r)], axis=2)
    blocks = blocks + pos_emb[None, None, :, None, :]
    flat = blocks.transpose(0, 1, 3, 2, 4).reshape(b_, nc, g_, CMP_BLOCK * d)
    return jax.nn.silu(flat @ w1) @ w2


def cmp_to_slc_overlap(nc, ns):
    c0 = jnp.arange(nc) * CMP_STRIDE
    s0 = jnp.arange(ns) * SLC_BLOCK
    lo = jnp.maximum(c0[:, None], s0[None, :])
    hi = jnp.minimum(c0[:, None] + CMP_BLOCK, s0[None, :] + SLC_BLOCK)
    return (jnp.maximum(hi - lo, 0) / CMP_BLOCK).astype(jnp.float32)


def odd_mixer(h, w_in, cmp_pos_k, cmp_pos_v, cmp_w1_k, cmp_w2_k, cmp_w1_v, cmp_w2_v, w_out):
    b_, t_, _ = h.shape
    H, G, d = NSA_HEADS, NSA_KV_GROUPS, NSA_HD
    R = H // G
    q, kc, vc, ks, vs, kw, vw, gl = split_cols(h @ w_in, ODD_COLS)
    q = q.reshape(b_, t_, H, d) * (d ** -0.5)
    kvh = lambda a: a.reshape(b_, t_, G, d)
    k_cmp = compress_blocks(kvh(kc), cmp_pos_k, cmp_w1_k, cmp_w2_k)
    v_cmp = compress_blocks(kvh(vc), cmp_pos_v, cmp_w1_v, cmp_w2_v)
    nc = k_cmp.shape[1]
    ns = t_ // SLC_BLOCK
    n_sel = min(SLC_TOPN, ns)
    overlap = cmp_to_slc_overlap(nc, ns)
    c_end = jnp.arange(nc) * CMP_STRIDE + CMP_BLOCK - 1
    k_sb = kvh(ks).reshape(b_, ns, SLC_BLOCK, G, d).transpose(0, 3, 1, 2, 4)
    v_sb = kvh(vs).reshape(b_, ns, SLC_BLOCK, G, d).transpose(0, 3, 1, 2, 4)
    kw_pad = jnp.pad(kvh(kw), ((0, 0), (WINDOW, 0), (0, 0), (0, 0)))
    vw_pad = jnp.pad(kvh(vw), ((0, 0), (WINDOW, 0), (0, 0), (0, 0)))
    slopes = alibi_slopes(H).reshape(G, R)
    gates = jax.nn.sigmoid(gl.astype(jnp.float32)).reshape(b_, t_, H, 3)
    gather = jax.vmap(jax.vmap(lambda tbl, ix: tbl[ix]))
    nq = t_ // Q_BLOCK
    f32 = jnp.float32

    def block(args):
        n, qb, gb = args
        qg = qb.reshape(b_, Q_BLOCK, G, R, d)
        t = n * Q_BLOCK + jnp.arange(Q_BLOCK)
        s_c = jnp.einsum('bqgrd,bcgd->bgrqc', qg, k_cmp, preferred_element_type=f32)
        dist_c = (t[:, None] - c_end[None, :]).astype(f32)
        valid_c = dist_c >= 0
        s_c = jnp.where(valid_c, s_c - slopes[None, :, :, None, None] * dist_c, NEG_INF)
        p_c = jax.nn.softmax(s_c, axis=-1) * valid_c
        o_c = jnp.einsum('bgrqc,bcgd->bqgrd', p_c, v_cmp.astype(f32))
        p_slc = jnp.einsum('bgrqc,cj->bgqj', p_c, overlap)
        qblk = t // SLC_BLOCK
        js = jnp.arange(ns)
        forced = (js[None, :] == 0) | (js[None, :] == qblk[:, None]) | (js[None, :] == qblk[:, None] - 1)
        allowed = js[None, :] <= qblk[:, None]
        score = jnp.where(forced, FORCE_SCORE, jnp.where(allowed, p_slc, NEG_INF))
        _, idx = lax.top_k(score, n_sel)
        kg = gather(k_sb, idx)
        vg = gather(v_sb, idx)
        s_s = jnp.einsum('bqgrd,bgqnld->bgrqnl', qg, kg, preferred_element_type=f32)
        spos = idx[..., None] * SLC_BLOCK + jnp.arange(SLC_BLOCK)
        dist_s = (t[None, None, :, None, None] - spos).astype(f32)[:, :, None]
        s_s = jnp.where(dist_s >= 0, s_s - slopes[None, :, :, None, None, None] * dist_s, NEG_INF)
        p_s = jax.nn.softmax(s_s.reshape(b_, G, R, Q_BLOCK, -1), axis=-1).reshape(s_s.shape)
        o_s = jnp.einsum('bgrqnl,bgqnld->bqgrd', p_s, vg.astype(f32))
        kwb = lax.dynamic_slice_in_dim(kw_pad, n * Q_BLOCK, WINDOW + Q_BLOCK, axis=1)
        vwb = lax.dynamic_slice_in_dim(vw_pad, n * Q_BLOCK, WINDOW + Q_BLOCK, axis=1)
        wpos = n * Q_BLOCK - WINDOW + jnp.arange(WINDOW + Q_BLOCK)
        dist_w = t[:, None] - wpos[None, :]
        valid_w = (dist_w >= 0) & (dist_w < WINDOW) & (wpos[None, :] >= 0)
        s_w = jnp.einsum('bqgrd,bkgd->bgrqk', qg, kwb, preferred_element_type=f32)
        s_w = jnp.where(valid_w, s_w - slopes[None, :, :, None, None] * dist_w.astype(f32), NEG_INF)
        p_w = jax.nn.softmax(s_w, axis=-1)
        o_w = jnp.einsum('bgrqk,bkgd->bqgrd', p_w, vwb.astype(f32))
        g5 = gb.reshape(b_, Q_BLOCK, G, R, 3)
        o = g5[..., 0:1] * o_c + g5[..., 1:2] * o_s + g5[..., 2:3] * o_w
        return o.reshape(b_, Q_BLOCK, H * d)

    qblocks = q.reshape(b_, nq, Q_BLOCK, H, d).transpose(1, 0, 2, 3, 4)
    gblocks = gates.reshape(b_, nq, Q_BLOCK, H, 3).transpose(1, 0, 2, 3, 4)
    out = lax.map(block, (jnp.arange(nq), qblocks, gblocks))
    out = out.transpose(1, 0, 2, 3).reshape(b_, t_, H * d)
    return out.astype(h.dtype) @ w_out


def swiglu(h, w_gate_up, w_down):
    g, u = jnp.split(h @ w_gate_up, 2, axis=-1)
    return (jax.nn.silu(g) * u) @ w_down


def setup_inputs(seed: int = 0) -> dict:
    key = jax.random.key(seed)
    ks = jax.random.split(key, 20)
    f32 = jnp.float32
    w = lambda k, shape, fan_in: jax.random.normal(k, shape, f32) * (fan_in ** -0.5)
    gain = lambda k, shape: 1.0 + 0.02 * jax.random.normal(k, shape, f32)
    even_in = sum(EVEN_COLS)
    odd_in = sum(ODD_COLS)
    return {
        'x': jax.random.normal(ks[0], (BATCH, SEQ, D_MODEL), f32),
        'mix_norm': gain(ks[1], (DEPTH, D_MODEL)),
        'ffn_norm': gain(ks[2], (DEPTH, D_MODEL)),
        'final_norm': gain(ks[3], (D_MODEL,)),
        'even_w_in': w(ks[4], (N_EVEN, D_MODEL, even_in), D_MODEL),
        'hgrn_lower_bounds': 0.1 * jax.random.normal(ks[5], (N_EVEN + 1, HGRN_HEADS * HGRN_DK), f32),
        'hgrn_out_norm': gain(ks[6], (N_EVEN, HGRN_DV)),
        'ret_out_norm': gain(ks[7], (N_EVEN, RET_DV)),
        'even_w_out': w(ks[8], (N_EVEN, EVEN_MIX_OUT, D_MODEL), EVEN_MIX_OUT),
        'odd_w_in': w(ks[9], (N_ODD, D_MODEL, odd_in), D_MODEL),
        'cmp_pos_k': 0.1 * jax.random.normal(ks[10], (N_ODD, CMP_BLOCK, NSA_HD), f32),
        'cmp_pos_v': 0.1 * jax.random.normal(ks[11], (N_ODD, CMP_BLOCK, NSA_HD), f32),
        'cmp_w1_k': w(ks[12], (N_ODD, CMP_BLOCK * NSA_HD, NSA_HD), CMP_BLOCK * NSA_HD),
        'cmp_w2_k': w(ks[13], (N_ODD, NSA_HD, NSA_HD), NSA_HD),
        'cmp_w1_v': w(ks[14], (N_ODD, CMP_BLOCK * NSA_HD, NSA_HD), CMP_BLOCK * NSA_HD),
        'cmp_w2_v': w(ks[15], (N_ODD, NSA_HD, NSA_HD), NSA_HD),
        'odd_w_out': w(ks[16], (N_ODD, NSA_HEADS * NSA_HD, D_MODEL), NSA_HEADS * NSA_HD),
        'ffn_w_gate_up': w(ks[17], (DEPTH, D_MODEL, 2 * FFN_HIDDEN), D_MODEL),
        'ffn_w_down': w(ks[18], (DEPTH, FFN_HIDDEN, D_MODEL), FFN_HIDDEN),
    }


def reference(x, mix_norm, ffn_norm, final_norm, even_w_in, hgrn_lower_bounds, hgrn_out_norm,
              ret_out_norm, even_w_out, odd_w_in, cmp_pos_k, cmp_pos_v, cmp_w1_k, cmp_w2_k,
              cmp_w1_v, cmp_w2_v, odd_w_out, ffn_w_gate_up, ffn_w_down):
    lb_all = jnp.cumsum(jax.nn.softmax(hgrn_lower_bounds.astype(jnp.float32), axis=0), axis=0)
    h = x
    for layer in range(DEPTH):
        hn = rmsnorm(h, mix_norm[layer])
        if layer % 2 == 0:
            e = layer // 2
            h = h + even_mixer(hn, even_w_in[e], lb_all[e], hgrn_out_norm[e], ret_out_norm[e], even_w_out[e])
        else:
            o = layer // 2
            h = h + odd_mixer(hn, odd_w_in[o], cmp_pos_k[o], cmp_pos_v[o], cmp_w1_k[o], cmp_w2_k[o],
                              cmp_w1_v[o], cmp_w2_v[o], odd_w_out[o])
        h = h + swiglu(rmsnorm(h, ffn_norm[layer]), ffn_w_gate_up[layer], ffn_w_down[layer])
    return rmsnorm(h, final_norm)
```

```python
import functools

import numpy as np
import jax
import jax.numpy as jnp
from jax import lax
from jax.experimental import pallas as pl
from jax.experimental.pallas import tpu as pltpu

F32 = jnp.float32
BF16 = jnp.bfloat16

D_MODEL = 1024
RMS_EPS = 1e-6
NEG_INF = -1e30
FORCE_SCORE = 1e9

LIN_HEADS = 4
LIN_D = 128
LIN_CHUNK = 64
SUB = 16
EVEN_IN = 8 * LIN_HEADS * LIN_D

NSA_HD = 64
NSA_HEADS = 16
NSA_G = 2
NSA_R = NSA_HEADS // NSA_G
CMP_BLOCK = 32
CMP_STRIDE = 16
SLC_BLOCK = 64
SLC_TOPN = 16
WINDOW = 512
QB = 128
KT = 128
HQ = NSA_R * QB
ODD_IN = NSA_HEADS * NSA_HD + 6 * NSA_G * NSA_HD + 3 * NSA_HEADS
ODD_IN_PAD = 1920

FFN_HIDDEN = 2816

VMEM_LIMIT = 56 * 1024 * 1024

_NT = (((1,), (1,)), ((), ()))
_TN = (((0,), (0,)), ((), ()))


def _cparams(sem):
    return pltpu.CompilerParams(dimension_semantics=sem, vmem_limit_bytes=VMEM_LIMIT)


def _rms(x, g):
    ms = jnp.mean(x * x, axis=-1, keepdims=True)
    return x * lax.rsqrt(ms + RMS_EPS) * g


def _silu(x):
    return x * jax.nn.sigmoid(x)


def _norm_matmul_kernel(x_ref, g_ref, w_ref, o_ref, xn_ref):
    @pl.when(pl.program_id(1) == 0)
    def _():
        xn_ref[...] = _rms(x_ref[...], g_ref[...]).astype(BF16)

    o_ref[...] = jnp.dot(xn_ref[...], w_ref[...], preferred_element_type=F32).astype(o_ref.dtype)


def norm_matmul(x, g, w, tm, tn, out_dtype=F32):
    m, k = x.shape
    n = w.shape[1]
    return pl.pallas_call(
        _norm_matmul_kernel,
        grid=(m // tm, n // tn),
        in_specs=[
            pl.BlockSpec((tm, k), lambda i, j: (i, 0)),
            pl.BlockSpec((1, k), lambda i, j: (0, 0)),
            pl.BlockSpec((k, tn), lambda i, j: (0, j)),
        ],
        out_specs=pl.BlockSpec((tm, tn), lambda i, j: (i, j)),
        out_shape=jax.ShapeDtypeStruct((m, n), out_dtype),
        scratch_shapes=[pltpu.VMEM((tm, k), BF16)],
        compiler_params=_cparams(("parallel", "arbitrary")),
    )(x, g.reshape(1, k), w)


def _matmul_res_kernel(a_ref, w_ref, r_ref, o_ref):
    o_ref[...] = r_ref[...] + jnp.dot(a_ref[...], w_ref[...], preferred_element_type=F32)


def matmul_residual(a, w, res, tm):
    m, k = a.shape
    n = w.shape[1]
    return pl.pallas_call(
        _matmul_res_kernel,
        grid=(m // tm,),
        in_specs=[
            pl.BlockSpec((tm, k), lambda i: (i, 0)),
            pl.BlockSpec((k, n), lambda i: (0, 0)),
            pl.BlockSpec((tm, n), lambda i: (i, 0)),
        ],
        out_specs=pl.BlockSpec((tm, n), lambda i: (i, 0)),
        out_shape=jax.ShapeDtypeStruct((m, n), F32),
        compiler_params=_cparams(("parallel",)),
    )(a, w, res)


def _ffn_kernel(x_ref, g_ref, wg_ref, wu_ref, wd_ref, fg_ref, o_ref, xn_ref, acc_ref, *, final_norm):
    j = pl.program_id(1)

    @pl.when(j == 0)
    def _():
        xn_ref[...] = _rms(x_ref[...], g_ref[...]).astype(BF16)
        acc_ref[...] = jnp.zeros_like(acc_ref)

    xn = xn_ref[...]
    gate = jnp.dot(xn, wg_ref[...], preferred_element_type=F32)
    up = jnp.dot(xn, wu_ref[...], preferred_element_type=F32)
    a = (_silu(gate) * up).astype(BF16)
    acc_ref[...] += jnp.dot(a, wd_ref[...], preferred_element_type=F32)

    @pl.when(j == pl.num_programs(1) - 1)
    def _():
        h = x_ref[...] + acc_ref[...]
        if final_norm:
            h = _rms(h, fg_ref[...])
        o_ref[...] = h


def ffn(x, g, w_gate_up, w_down, final_g, tm, th):
    m, k = x.shape
    hid = w_down.shape[0]
    nj = hid // th
    final_norm = final_g is not None
    fg = final_g if final_norm else g
    return pl.pallas_call(
        functools.partial(_ffn_kernel, final_norm=final_norm),
        grid=(m // tm, nj),
        in_specs=[
            pl.BlockSpec((tm, k), lambda i, j: (i, 0)),
            pl.BlockSpec((1, k), lambda i, j: (0, 0)),
            pl.BlockSpec((k, th), lambda i, j: (0, j)),
            pl.BlockSpec((k, th), lambda i, j: (0, j + nj)),
            pl.BlockSpec((th, k), lambda i, j: (j, 0)),
            pl.BlockSpec((1, k), lambda i, j: (0, 0)),
        ],
        out_specs=pl.BlockSpec((tm, k), lambda i, j: (i, 0)),
        out_shape=jax.ShapeDtypeStruct((m, k), F32),
        scratch_shapes=[pltpu.VMEM((tm, k), BF16), pltpu.VMEM((tm, k), F32)],
        compiler_params=_cparams(("parallel", "arbitrary")),
    )(x, g.reshape(1, k), w_gate_up, w_gate_up, w_down, fg.reshape(1, k))


def _cumsum_rows(g):
    c = g.shape[0]
    row = lax.broadcasted_iota(jnp.int32, g.shape, 0)
    s = 1
    while s < c:
        g = g + jnp.where(row >= s, pltpu.roll(g, s, axis=0), 0.0)
        s *= 2
    return g


def _hgrn_chunk(hq, hf, hi, lb, st_ref, h, o_scr):
    c = LIN_CHUNK
    f = lb + (1.0 - lb) * jax.nn.sigmoid(hf)
    k = 1.0 - f
    q = _silu(hq)
    v = hi
    cum = _cumsum_rows(jnp.log(f))
    st = st_ref[h]
    o_inter = lax.dot_general((q * jnp.exp(cum)).astype(BF16), st.astype(BF16), _NT,
                              preferred_element_type=F32)
    jrow = lax.broadcasted_iota(jnp.int32, (SUB, LIN_D), 0)
    vb = v.astype(BF16)
    for a in range(c // SUB):
        lo = a * SUB
        cum_a = cum[lo:lo + SUB]
        q_a = q[lo:lo + SUB]
        k_a = k[lo:lo + SUB]
        v_a = v[lo:lo + SUB]
        for i in range(SUB):
            diff = cum_a[i:i + 1, :] - cum_a
            dec = jnp.exp(jnp.where(jrow <= i, diff, NEG_INF))
            w = (q_a[i:i + 1, :] * k_a) * dec
            s = jnp.sum(w, axis=-1, keepdims=True)
            o_scr[lo + i:lo + i + 1, :] = jnp.sum(s * v_a, axis=0, keepdims=True)
        if a > 0:
            ref = cum[lo - 1:lo, :]
            qd = (q_a * jnp.exp(cum_a - ref)).astype(BF16)
            kd = (k[:lo] * jnp.exp(ref - cum[:lo])).astype(BF16)
            attn = lax.dot_general(qd, kd, _NT, preferred_element_type=F32)
            o_scr[lo:lo + SUB, :] += jnp.dot(attn.astype(BF16), vb[:lo], preferred_element_type=F32)
    last = cum[c - 1:c, :]
    kd = (k * jnp.exp(last - cum)).astype(BF16)
    st_ref[h] = st * jnp.exp(last) + lax.dot_general(vb, kd, _TN, preferred_element_type=F32)
    return o_scr[...] + o_inter


def _ret_chunk(rq, rk, rv, dec, qdec, kdec, cdec, st_ref, h):
    k = rk * (LIN_D ** -0.5)
    st = st_ref[h]
    qb = rq.astype(BF16)
    vb = rv.astype(BF16)
    attn = lax.dot_general(qb, k.astype(BF16), _NT, preferred_element_type=F32) * dec
    o = jnp.dot(attn.astype(BF16), vb, preferred_element_type=F32)
    o = o + lax.dot_general((rq * qdec).astype(BF16), st.astype(BF16), _NT, preferred_element_type=F32)
    st_ref[h] = st * cdec + lax.dot_general(vb, (k * kdec).astype(BF16), _TN, preferred_element_type=F32)
    return o


def _even_mix_kernel(p_ref, lb_ref, hn_ref, rn_ref, dec_ref, qdec_ref, kdec_ref, cdec_ref,
                     o_ref, st_ref, o_scr, *, tb):
    @pl.when(pl.program_id(1) == 0)
    def _():
        st_ref[...] = jnp.zeros_like(st_ref)

    w = LIN_HEADS * LIN_D

    def chunk(ci, carry):
        rows = pl.ds(pl.multiple_of(ci * LIN_CHUNK, LIN_CHUNK), LIN_CHUNK)
        for h in range(LIN_HEADS):
            col = lambda part: p_ref[0, rows, part * w + h * LIN_D:part * w + (h + 1) * LIN_D]
            o = _hgrn_chunk(col(0), col(1), col(2), lb_ref[:, h * LIN_D:(h + 1) * LIN_D], st_ref, h, o_scr)
            o = _rms(o, hn_ref[...]) * _silu(col(3))
            o_ref[0, rows, h * LIN_D:(h + 1) * LIN_D] = o.astype(o_ref.dtype)
        for h in range(LIN_HEADS):
            col = lambda part: p_ref[0, rows, (4 + part) * w + h * LIN_D:(4 + part) * w + (h + 1) * LIN_D]
            o = _ret_chunk(col(0), col(1), col(2), dec_ref[h], qdec_ref[h], kdec_ref[h], cdec_ref[h],
                           st_ref, LIN_HEADS + h)
            o = _rms(o, rn_ref[...]) * _silu(col(3))
            o_ref[0, rows, w + h * LIN_D:w + (h + 1) * LIN_D] = o.astype(o_ref.dtype)
        return carry

    lax.fori_loop(0, tb // LIN_CHUNK, chunk, 0)


def even_mix(proj, lb, hgrn_norm, ret_norm, tb):
    b, t, _ = proj.shape
    c = LIN_CHUNK
    log_gamma = jnp.log(1.0 - jnp.exp2(-5.0 - jnp.arange(LIN_HEADS, dtype=F32)))
    pos = jnp.arange(c, dtype=F32)
    rel = pos[:, None] - pos[None, :]
    dec = jnp.where(rel[None] >= 0, jnp.exp(jnp.maximum(rel, 0.0)[None] * log_gamma[:, None, None]), 0.0)
    ones = jnp.ones((1, 1, LIN_D), F32)
    qdec = jnp.exp((pos + 1.0)[None, :] * log_gamma[:, None])[..., None] * ones
    kdec = jnp.exp((c - 1.0 - pos)[None, :] * log_gamma[:, None])[..., None] * ones
    cdec = jnp.exp(c * log_gamma)[:, None, None] * ones
    const = lambda shape: pl.BlockSpec(shape, lambda i, j: (0,) * len(shape))
    return pl.pallas_call(
        functools.partial(_even_mix_kernel, tb=tb),
        grid=(b, t // tb),
        in_specs=[
            pl.BlockSpec((1, tb, EVEN_IN), lambda i, j: (i, j, 0)),
            const((1, LIN_HEADS * LIN_D)),
            const((1, LIN_D)),
            const((1, LIN_D)),
            const((LIN_HEADS, c, c)),
            const((LIN_HEADS, c, LIN_D)),
            const((LIN_HEADS, c, LIN_D)),
            const((LIN_HEADS, 1, LIN_D)),
        ],
        out_specs=pl.BlockSpec((1, tb, 2 * LIN_HEADS * LIN_D), lambda i, j: (i, j, 0)),
        out_shape=jax.ShapeDtypeStruct((b, t, 2 * LIN_HEADS * LIN_D), BF16),
        scratch_shapes=[pltpu.VMEM((2 * LIN_HEADS, LIN_D, LIN_D), F32), pltpu.VMEM((c, LIN_D), F32)],
        compiler_params=_cparams(("parallel", "arbitrary")),
    )(proj, lb.reshape(1, -1), hgrn_norm.reshape(1, -1), ret_norm.reshape(1, -1), dec, qdec, kdec, cdec)


def _compress_kernel(ck_ref, cv_ref, pk_ref, pv_ref, w1k_ref, w1v_ref, w2k_ref, w2v_ref, ok_ref, ov_ref):
    half = CMP_STRIDE * NSA_HD

    def one(c_ref, p_ref, w1_ref, w2_ref, o_ref):
        ch = c_ref[0, 0]
        a = jnp.dot((ch + p_ref[:, :half]).astype(BF16), w1_ref[:half, :], preferred_element_type=F32)
        b = jnp.dot((ch + p_ref[:, half:]).astype(BF16), w1_ref[half:, :], preferred_element_type=F32)
        nch = a.shape[0]
        pre = a + pltpu.roll(b, nch - 1, axis=0)
        o_ref[0, 0] = jnp.dot(_silu(pre).astype(BF16), w2_ref[...], preferred_element_type=F32).astype(o_ref.dtype)

    one(ck_ref, pk_ref, w1k_ref, w2k_ref, ok_ref)
    one(cv_ref, pv_ref, w1v_ref, w2v_ref, ov_ref)


def compress(ck, cv, pos_k, pos_v, w1k, w1v, w2k, w2v):
    b, g, nch, width = ck.shape
    blk = pl.BlockSpec((1, 1, nch, width), lambda i, j: (i, j, 0, 0))
    const = lambda shape: pl.BlockSpec(shape, lambda i, j: (0,) * len(shape))
    oblk = pl.BlockSpec((1, 1, nch, NSA_HD), lambda i, j: (i, j, 0, 0))
    oshape = jax.ShapeDtypeStruct((b, g, nch, NSA_HD), BF16)
    return pl.pallas_call(
        _compress_kernel,
        grid=(b, g),
        in_specs=[blk, blk, const((1, 2 * width)), const((1, 2 * width)),
                  const((2 * width, NSA_HD)), const((2 * width, NSA_HD)),
                  const((NSA_HD, NSA_HD)), const((NSA_HD, NSA_HD))],
        out_specs=[oblk, oblk],
        out_shape=[oshape, oshape],
        compiler_params=_cparams(("parallel", "parallel")),
    )(ck, cv, pos_k.reshape(1, -1), pos_v.reshape(1, -1), w1k, w1v, w2k, w2v)


def _cmp_attn_kernel(q_ref, kc_ref, vc_ref, slope_ref, ovt_ref, oc_ref, sel_ref, flag_ref):
    n = pl.program_id(2)
    nc = kc_ref.shape[2]
    ns = ovt_ref.shape[0]
    q = q_ref[0, 0, 0]
    s = lax.dot_general(q, kc_ref[0, 0], _NT, preferred_element_type=F32)
    tq = lax.broadcasted_iota(jnp.int32, (HQ, nc), 0) & (QB - 1)
    c_end = lax.broadcasted_iota(jnp.int32, (HQ, nc), 1) * CMP_STRIDE + (CMP_BLOCK - 1)
    dist = (n * QB + tq - c_end).astype(F32)
    valid = dist >= 0
    s = jnp.where(valid, s - slope_ref[0] * dist, NEG_INF)
    m = jnp.max(s, axis=-1, keepdims=True)
    e = jnp.where(valid, jnp.exp(s - m), 0.0)
    l = jnp.sum(e, axis=-1, keepdims=True)
    p = e * (1.0 / jnp.where(l > 0, l, 1.0))
    oc_ref[0, 0, 0] = jnp.dot(p.astype(BF16), vc_ref[0, 0], preferred_element_type=F32).astype(oc_ref.dtype)

    psum = p[0:QB]
    for r in range(1, NSA_R):
        psum = psum + p[r * QB:(r + 1) * QB]
    hi = psum.astype(BF16)
    lo = (psum - hi.astype(F32)).astype(BF16)
    ovt = ovt_ref[...]
    pslc = (lax.dot_general(ovt, hi, _NT, preferred_element_type=F32)
            + lax.dot_general(ovt, lo, _NT, preferred_element_type=F32))

    jj = lax.broadcasted_iota(jnp.int32, (ns, QB), 0)
    qblk = (n * QB + lax.broadcasted_iota(jnp.int32, (ns, QB), 1)) >> 6
    forced = (jj == 0) | (jj == qblk) | (jj == qblk - 1)
    score = jnp.where(forced, FORCE_SCORE, jnp.where(jj <= qblk, pslc, NEG_INF))
    sel = jnp.zeros((ns, QB), F32)
    for _ in range(min(SLC_TOPN, ns)):
        mx = jnp.max(score, axis=0, keepdims=True)
        idx = jnp.min(jnp.where(score == mx, jj, ns), axis=0, keepdims=True)
        pick = jj == idx
        sel = jnp.where(pick, 1.0, sel)
        score = jnp.where(pick, -jnp.inf, score)
    sel_q = sel.T
    sel_ref[0, 0, 0] = sel_q
    flag_ref[0, 0, 0] = jnp.max(sel_q, axis=0, keepdims=True).astype(jnp.int32)


def cmp_attention(q_all, k_cmp, v_cmp, slope_col, ovt):
    b, g, nq, _, d = q_all.shape
    nc = k_cmp.shape[2]
    ns = ovt.shape[0]
    tile = lambda shape: pl.BlockSpec((1, 1, 1) + shape, lambda i, j, n: (i, j, n, 0, 0))
    per_bg = pl.BlockSpec((1, 1, nc, d), lambda i, j, n: (i, j, 0, 0))
    return pl.pallas_call(
        _cmp_attn_kernel,
        grid=(b, g, nq),
        in_specs=[
            tile((HQ, d)), per_bg, per_bg,
            pl.BlockSpec((1, HQ, 1), lambda i, j, n: (j, 0, 0)),
            pl.BlockSpec((ns, nc), lambda i, j, n: (0, 0)),
        ],
        out_specs=[tile((HQ, d)), tile((QB, ns)), tile((1, ns))],
        out_shape=[
            jax.ShapeDtypeStruct((b, g, nq, HQ, d), BF16),
            jax.ShapeDtypeStruct((b, g, nq, QB, ns), F32),
            jax.ShapeDtypeStruct((b, g, nq, 1, ns), jnp.int32),
        ],
        compiler_params=_cparams(("parallel", "parallel", "arbitrary")),
    )(q_all, k_cmp, v_cmp, slope_col, ovt)


def _flash_step(q, kb, vb, bias, mask, m_ref, l_ref, acc_ref):
    s = lax.dot_general(q, kb, _NT, preferred_element_type=F32) - bias
    s = jnp.where(mask, s, NEG_INF)
    m_old = m_ref[...]
    m_new = jnp.maximum(m_old, jnp.max(s, axis=-1, keepdims=True))
    alpha = jnp.exp(m_old - m_new)
    p = jnp.exp(s - m_new)
    l_ref[...] = alpha * l_ref[...] + jnp.sum(p, axis=-1, keepdims=True)
    acc_ref[...] = alpha * acc_ref[...] + jnp.dot(p.astype(BF16), vb, preferred_element_type=F32)
    m_ref[...] = m_new


def _sel_win_kernel(flags_ref, q_ref, ks_ref, vs_ref, kw_ref, vw_ref, sel_ref, oc_ref, gl_ref, slope_ref,
                    o_ref, m_ref, l_ref, acc_ref, ms_ref, ls_ref, accs_ref, *, nq):
    bi, gi, n = pl.program_id(0), pl.program_id(1), pl.program_id(2)
    ns = sel_ref.shape[4]
    q = q_ref[0, 0, 0]
    slope = slope_ref[0]
    tq = lax.broadcasted_iota(jnp.int32, (HQ, KT), 0) & (QB - 1)
    lane = lax.broadcasted_iota(jnp.int32, (HQ, KT), 1)
    rel = tq - lane
    rel_bias = slope * rel.astype(F32)

    def init(m, l, acc):
        m[...] = jnp.full(m.shape, NEG_INF, F32)
        l[...] = jnp.zeros(l.shape, F32)
        acc[...] = jnp.zeros(acc.shape, F32)

    init(ms_ref, ls_ref, accs_ref)
    sel = sel_ref[0, 0, 0]
    jlane = lax.broadcasted_iota(jnp.int32, (QB, ns), 1)
    fbase = ((bi * NSA_G + gi) * nq + n) * (ns // 2)

    def sel_body(p, carry):
        @pl.when(flags_ref[fbase + p] != 0)
        def _():
            rows = pl.ds(pl.multiple_of(p * KT, KT), KT)
            c0 = jnp.sum(jnp.where(jlane == 2 * p, sel, 0.0), axis=-1, keepdims=True)
            c1 = jnp.sum(jnp.where(jlane == 2 * p + 1, sel, 0.0), axis=-1, keepdims=True)
            c0 = jnp.concatenate([c0] * NSA_R, axis=0)
            c1 = jnp.concatenate([c1] * NSA_R, axis=0)
            chosen = jnp.where(lane < SLC_BLOCK, c0, c1) > 0.5
            base = ((n - p) * KT).astype(F32)
            mask = chosen & (rel + (n - p) * KT >= 0)
            _flash_step(q, ks_ref[0, 0, rows, :], vs_ref[0, 0, rows, :], rel_bias + slope * base, mask,
                        ms_ref, ls_ref, accs_ref)
        return carry

    lax.fori_loop(0, n + 1, sel_body, 0)

    init(m_ref, l_ref, acc_ref)

    def win_body(p, carry):
        rows = pl.ds(pl.multiple_of(p * KT, KT), KT)
        dist = rel + (n - p) * KT
        mask = (dist >= 0) & (dist < WINDOW)
        base = ((n - p) * KT).astype(F32)
        _flash_step(q, kw_ref[0, 0, rows, :], vw_ref[0, 0, rows, :], rel_bias + slope * base, mask,
                    m_ref, l_ref, acc_ref)
        return carry

    lax.fori_loop(jnp.maximum(n - WINDOW // KT, 0), n + 1, win_body, 0)

    gates = jax.nn.sigmoid(gl_ref[0, 0, 0])
    o_s = accs_ref[...] * (1.0 / ls_ref[...])
    o_w = acc_ref[...] * (1.0 / l_ref[...])
    o = gates[:, 0:1] * oc_ref[0, 0, 0].astype(F32) + gates[:, 1:2] * o_s + gates[:, 2:3] * o_w
    o_ref[0, 0, 0] = o.astype(o_ref.dtype)


def sel_win_attention(pair_flags, q_all, ks, vs, kw, vw, sel, o_c, gl, slope_col):
    b, g, nq, _, d = q_all.shape
    t = ks.shape[2]
    ns = sel.shape[4]
    tile = lambda shape: pl.BlockSpec((1, 1, 1) + shape, lambda i, j, n, f: (i, j, n, 0, 0))
    per_bg = pl.BlockSpec((1, 1, t, d), lambda i, j, n, f: (i, j, 0, 0))
    grid_spec = pltpu.PrefetchScalarGridSpec(
        num_scalar_prefetch=1,
        grid=(b, g, nq),
        in_specs=[
            tile((HQ, d)), per_bg, per_bg, per_bg, per_bg,
            tile((QB, ns)), tile((HQ, d)), tile((HQ, 3)),
            pl.BlockSpec((1, HQ, 1), lambda i, j, n, f: (j, 0, 0)),
        ],
        out_specs=tile((HQ, d)),
        scratch_shapes=[pltpu.VMEM((HQ, 1), F32), pltpu.VMEM((HQ, 1), F32), pltpu.VMEM((HQ, d), F32),
                        pltpu.VMEM((HQ, 1), F32), pltpu.VMEM((HQ, 1), F32), pltpu.VMEM((HQ, d), F32)],
    )
    return pl.pallas_call(
        functools.partial(_sel_win_kernel, nq=nq),
        grid_spec=grid_spec,
        out_shape=jax.ShapeDtypeStruct((b, g, nq, HQ, d), BF16),
        compiler_params=_cparams(("parallel", "parallel", "arbitrary")),
    )(pair_flags, q_all, ks, vs, kw, vw, sel, o_c, gl, slope_col)


def _row_tile(m):
    return 512 if m % 512 == 0 else m


def even_layer_mixer(h, norm_g, w_in, lb, hgrn_norm, ret_norm, w_out):
    b, t, dm = h.shape
    x2 = h.reshape(b * t, dm)
    tm = _row_tile(b * t)
    proj = norm_matmul(x2, norm_g, w_in.astype(BF16), tm, 1024)
    o = even_mix(proj.reshape(b, t, EVEN_IN), lb, hgrn_norm, ret_norm, min(t, 256))
    return matmul_residual(o.reshape(b * t, -1), w_out.astype(BF16), x2, tm).reshape(b, t, dm)


def odd_layer_mixer(h, norm_g, w_in, cmp_pos_k, cmp_pos_v, cmp_w1_k, cmp_w2_k, cmp_w1_v, cmp_w2_v, w_out):
    b, t, dm = h.shape
    g_, r_, d = NSA_G, NSA_R, NSA_HD
    nq = t // QB
    ns = t // SLC_BLOCK
    nch = t // CMP_STRIDE
    x2 = h.reshape(b * t, dm)
    tm = _row_tile(b * t)
    w_pad = jnp.pad(w_in, ((0, 0), (0, ODD_IN_PAD - ODD_IN))).astype(BF16)
    proj = norm_matmul(x2, norm_g, w_pad, tm, ODD_IN_PAD // 3)
    kvw = g_ * d
    o0 = NSA_HEADS * d
    q = proj[:, :o0] * (d ** -0.5)
    part = lambda i: proj[:, o0 + i * kvw:o0 + (i + 1) * kvw]
    gl = proj[:, o0 + 6 * kvw:o0 + 6 * kvw + 3 * NSA_HEADS]

    q_all = q.reshape(b, nq, QB, g_, r_, d).transpose(0, 3, 1, 4, 2, 5).reshape(b, g_, nq, HQ, d).astype(BF16)
    gl_all = gl.reshape(b, nq, QB, g_, r_, 3).transpose(0, 3, 1, 4, 2, 5).reshape(b, g_, nq, HQ, 3)
    per_group = lambda a: a.reshape(b, t, g_, d).transpose(0, 2, 1, 3)
    chunks = lambda a: a.reshape(b, nch, CMP_STRIDE, g_, d).transpose(0, 3, 1, 2, 4).reshape(b, g_, nch, CMP_STRIDE * d)
    ks, vs, kw, vw = (per_group(part(i)).astype(BF16) for i in (2, 3, 4, 5))

    k_cmp, v_cmp = compress(chunks(part(0)), chunks(part(1)), cmp_pos_k, cmp_pos_v,
                            cmp_w1_k.astype(BF16), cmp_w1_v.astype(BF16),
                            cmp_w2_k.astype(BF16), cmp_w2_v.astype(BF16))

    slopes = jnp.exp2(-8.0 * jnp.arange(1, NSA_HEADS + 1, dtype=F32) / NSA_HEADS).reshape(g_, r_)
    slope_col = jnp.repeat(slopes, QB, axis=1).reshape(g_, HQ, 1)
    c0 = np.arange(nch) * CMP_STRIDE
    s0 = np.arange(ns) * SLC_BLOCK
    ov = np.maximum(np.minimum(c0[:, None] + CMP_BLOCK, s0[None, :] + SLC_BLOCK)
                    - np.maximum(c0[:, None], s0[None, :]), 0) / CMP_BLOCK
    ov[nch - (CMP_BLOCK // CMP_STRIDE - 1):] = 0.0
    ovt = jnp.asarray(ov.T, BF16)

    o_c, sel, flags = cmp_attention(q_all, k_cmp, v_cmp, slope_col, ovt)
    pair_flags = flags.reshape(b, g_, nq, ns // 2, 2).max(axis=-1).reshape(-1)
    o = sel_win_attention(pair_flags, q_all, ks, vs, kw, vw, sel, o_c, gl_all, slope_col)
    o = o.reshape(b, g_, nq, r_, QB, d).transpose(0, 2, 4, 1, 3, 5).reshape(b * t, NSA_HEADS * d)
    return matmul_residual(o, w_out.astype(BF16), x2, tm).reshape(b, t, dm)


def ffn_layer(h, norm_g, w_gate_up, w_down, final_g):
    b, t, dm = h.shape
    out = ffn(h.reshape(b * t, dm), norm_g, w_gate_up.astype(BF16), w_down.astype(BF16), final_g,
              _row_tile(b * t), 256)
    return out.reshape(b, t, dm)


def kernel(x, mix_norm, ffn_norm, final_norm, even_w_in, hgrn_lower_bounds, hgrn_out_norm, ret_out_norm,
           even_w_out, odd_w_in, cmp_pos_k, cmp_pos_v, cmp_w1_k, cmp_w2_k, cmp_w1_v, cmp_w2_v, odd_w_out,
           ffn_w_gate_up, ffn_w_down):
    depth = mix_norm.shape[0]
    lb_all = jnp.cumsum(jax.nn.softmax(hgrn_lower_bounds.astype(F32), axis=0), axis=0)
    h = x
    for layer in range(depth):
        if layer % 2 == 0:
            e = layer // 2
            h = even_layer_mixer(h, mix_norm[layer], even_w_in[e], lb_all[e], hgrn_out_norm[e],
                                 ret_out_norm[e], even_w_out[e])
        else:
            o = layer // 2
            h = odd_layer_mixer(h, mix_norm[layer], odd_w_in[o], cmp_pos_k[o], cmp_pos_v[o], cmp_w1_k[o],
                                cmp_w2_k[o], cmp_w1_v[o], cmp_w2_v[o], odd_w_out[o])
        h = ffn_layer(h, ffn_norm[layer], ffn_w_gate_up[layer], ffn_w_down[layer],
                      final_norm if layer == depth - 1 else None)
    return h
```

```python
import functools

import numpy as np
import jax
import jax.numpy as jnp
from jax import lax
from jax.experimental import pallas as pl
from jax.experimental.pallas import tpu as pltpu

F32 = jnp.float32
BF16 = jnp.bfloat16

D_MODEL = 1024
RMS_EPS = 1e-6
NEG_INF = -1e30
FORCE_SCORE = 1e9

LIN_HEADS = 4
LIN_D = 128
LIN_CHUNK = 64
SUB = 16
EVEN_IN = 8 * LIN_HEADS * LIN_D

NSA_HD = 64
NSA_HEADS = 16
NSA_G = 2
NSA_R = NSA_HEADS // NSA_G
CMP_BLOCK = 32
CMP_STRIDE = 16
SLC_BLOCK = 64
SLC_TOPN = 16
WINDOW = 512
QB = 128
KT = 128
CT = 128
HQ = NSA_R * QB
ODD_IN = NSA_HEADS * NSA_HD + 6 * NSA_G * NSA_HD + 3 * NSA_HEADS
ODD_IN_PAD = 1920

FFN_HIDDEN = 2816

VMEM_LIMIT = 56 * 1024 * 1024

_NT = (((1,), (1,)), ((), ()))
_TN = (((0,), (0,)), ((), ()))


def _cparams(sem):
    return pltpu.CompilerParams(dimension_semantics=sem, vmem_limit_bytes=VMEM_LIMIT)


def _rms(x, g):
    ms = jnp.mean(x * x, axis=-1, keepdims=True)
    return x * lax.rsqrt(ms + RMS_EPS) * g


def _silu(x):
    return x * jax.nn.sigmoid(x)


def _norm_matmul_kernel(x_ref, g_ref, w_ref, o_ref, xn_ref):
    @pl.when(pl.program_id(1) == 0)
    def _():
        xn_ref[...] = _rms(x_ref[...], g_ref[...]).astype(BF16)

    o_ref[...] = jnp.dot(xn_ref[...], w_ref[...], preferred_element_type=F32).astype(o_ref.dtype)


def norm_matmul(x, g, w, tm, tn, out_dtype=F32):
    m, k = x.shape
    n = w.shape[1]
    return pl.pallas_call(
        _norm_matmul_kernel,
        grid=(m // tm, n // tn),
        in_specs=[
            pl.BlockSpec((tm, k), lambda i, j: (i, 0)),
            pl.BlockSpec((1, k), lambda i, j: (0, 0)),
            pl.BlockSpec((k, tn), lambda i, j: (0, j)),
        ],
        out_specs=pl.BlockSpec((tm, tn), lambda i, j: (i, j)),
        out_shape=jax.ShapeDtypeStruct((m, n), out_dtype),
        scratch_shapes=[pltpu.VMEM((tm, k), BF16)],
        compiler_params=_cparams(("parallel", "arbitrary")),
    )(x, g.reshape(1, k), w)


def _matmul_res_kernel(a_ref, w_ref, r_ref, o_ref):
    o_ref[...] = r_ref[...] + jnp.dot(a_ref[...], w_ref[...], preferred_element_type=F32)


def matmul_residual(a, w, res, tm):
    m, k = a.shape
    n = w.shape[1]
    return pl.pallas_call(
        _matmul_res_kernel,
        grid=(m // tm,),
        in_specs=[
            pl.BlockSpec((tm, k), lambda i: (i, 0)),
            pl.BlockSpec((k, n), lambda i: (0, 0)),
            pl.BlockSpec((tm, n), lambda i: (i, 0)),
        ],
        out_specs=pl.BlockSpec((tm, n), lambda i: (i, 0)),
        out_shape=jax.ShapeDtypeStruct((m, n), F32),
        compiler_params=_cparams(("parallel",)),
    )(a, w, res)


def _ffn_kernel(x_ref, g_ref, wg_ref, wu_ref, wd_ref, fg_ref, o_ref, xn_ref, acc_ref, *, final_norm):
    j = pl.program_id(1)

    @pl.when(j == 0)
    def _():
        xn_ref[...] = _rms(x_ref[...], g_ref[...]).astype(BF16)
        acc_ref[...] = jnp.zeros_like(acc_ref)

    xn = xn_ref[...]
    gate = jnp.dot(xn, wg_ref[...], preferred_element_type=F32)
    up = jnp.dot(xn, wu_ref[...], preferred_element_type=F32)
    a = (_silu(gate) * up).astype(BF16)
    acc_ref[...] += jnp.dot(a, wd_ref[...], preferred_element_type=F32)

    @pl.when(j == pl.num_programs(1) - 1)
    def _():
        h = x_ref[...] + acc_ref[...]
        if final_norm:
            h = _rms(h, fg_ref[...])
        o_ref[...] = h


def ffn(x, g, w_gate_up, w_down, final_g, tm, th):
    m, k = x.shape
    hid = w_down.shape[0]
    nj = hid // th
    final_norm = final_g is not None
    fg = final_g if final_norm else g
    return pl.pallas_call(
        functools.partial(_ffn_kernel, final_norm=final_norm),
        grid=(m // tm, nj),
        in_specs=[
            pl.BlockSpec((tm, k), lambda i, j: (i, 0)),
            pl.BlockSpec((1, k), lambda i, j: (0, 0)),
            pl.BlockSpec((k, th), lambda i, j: (0, j)),
            pl.BlockSpec((k, th), lambda i, j: (0, j + nj)),
            pl.BlockSpec((th, k), lambda i, j: (j, 0)),
            pl.BlockSpec((1, k), lambda i, j: (0, 0)),
        ],
        out_specs=pl.BlockSpec((tm, k), lambda i, j: (i, 0)),
        out_shape=jax.ShapeDtypeStruct((m, k), F32),
        scratch_shapes=[pltpu.VMEM((tm, k), BF16), pltpu.VMEM((tm, k), F32)],
        compiler_params=_cparams(("parallel", "arbitrary")),
    )(x, g.reshape(1, k), w_gate_up, w_gate_up, w_down, fg.reshape(1, k))


def _cumsum_rows(g):
    c = g.shape[0]
    row = lax.broadcasted_iota(jnp.int32, g.shape, 0)
    s = 1
    while s < c:
        g = g + jnp.where(row >= s, pltpu.roll(g, s, axis=0), 0.0)
        s *= 2
    return g


def _hgrn_chunk(hq, hf, hi, lb, st_ref, h, o_scr):
    c = LIN_CHUNK
    f = lb + (1.0 - lb) * jax.nn.sigmoid(hf)
    k = 1.0 - f
    q = _silu(hq)
    v = hi
    cum = _cumsum_rows(jnp.log(f))
    st = st_ref[h]
    o_inter = lax.dot_general((q * jnp.exp(cum)).astype(BF16), st.astype(BF16), _NT,
                              preferred_element_type=F32)
    jrow = lax.broadcasted_iota(jnp.int32, (SUB, LIN_D), 0)
    vb = v.astype(BF16)
    for a in range(c // SUB):
        lo = a * SUB
        cum_a = cum[lo:lo + SUB]
        q_a = q[lo:lo + SUB]
        k_a = k[lo:lo + SUB]
        v_a = v[lo:lo + SUB]
        for i in range(SUB):
            diff = cum_a[i:i + 1, :] - cum_a
            dec = jnp.exp(jnp.where(jrow <= i, diff, NEG_INF))
            w = (q_a[i:i + 1, :] * k_a) * dec
            s = jnp.sum(w, axis=-1, keepdims=True)
            o_scr[lo + i:lo + i + 1, :] = jnp.sum(s * v_a, axis=0, keepdims=True)
        if a > 0:
            ref = cum[lo - 1:lo, :]
            qd = (q_a * jnp.exp(cum_a - ref)).astype(BF16)
            kd = (k[:lo] * jnp.exp(ref - cum[:lo])).astype(BF16)
            attn = lax.dot_general(qd, kd, _NT, preferred_element_type=F32)
            o_scr[lo:lo + SUB, :] += jnp.dot(attn.astype(BF16), vb[:lo], preferred_element_type=F32)
    last = cum[c - 1:c, :]
    kd = (k * jnp.exp(last - cum)).astype(BF16)
    st_ref[h] = st * jnp.exp(last) + lax.dot_general(vb, kd, _TN, preferred_element_type=F32)
    return o_scr[...] + o_inter


def _ret_chunk(rq, rk, rv, dec, qdec, kdec, cdec, st_ref, h):
    k = rk * (LIN_D ** -0.5)
    st = st_ref[h]
    qb = rq.astype(BF16)
    vb = rv.astype(BF16)
    attn = lax.dot_general(qb, k.astype(BF16), _NT, preferred_element_type=F32) * dec
    o = jnp.dot(attn.astype(BF16), vb, preferred_element_type=F32)
    o = o + lax.dot_general((rq * qdec).astype(BF16), st.astype(BF16), _NT, preferred_element_type=F32)
    st_ref[h] = st * cdec + lax.dot_general(vb, (k * kdec).astype(BF16), _TN, preferred_element_type=F32)
    return o


def _even_mix_kernel(p_ref, lb_ref, hn_ref, rn_ref, dec_ref, qdec_ref, kdec_ref, cdec_ref,
                     o_ref, st_ref, o_scr, *, tb):
    @pl.when(pl.program_id(1) == 0)
    def _():
        st_ref[...] = jnp.zeros_like(st_ref)

    w = LIN_HEADS * LIN_D

    def chunk(ci, carry):
        rows = pl.ds(pl.multiple_of(ci * LIN_CHUNK, LIN_CHUNK), LIN_CHUNK)
        for h in range(LIN_HEADS):
            col = lambda part: p_ref[0, rows, part * w + h * LIN_D:part * w + (h + 1) * LIN_D]
            o = _hgrn_chunk(col(0), col(1), col(2), lb_ref[:, h * LIN_D:(h + 1) * LIN_D], st_ref, h, o_scr)
            o = _rms(o, hn_ref[...]) * _silu(col(3))
            o_ref[0, rows, h * LIN_D:(h + 1) * LIN_D] = o.astype(o_ref.dtype)
        for h in range(LIN_HEADS):
            col = lambda part: p_ref[0, rows, (4 + part) * w + h * LIN_D:(4 + part) * w + (h + 1) * LIN_D]
            o = _ret_chunk(col(0), col(1), col(2), dec_ref[h], qdec_ref[h], kdec_ref[h], cdec_ref[h],
                           st_ref, LIN_HEADS + h)
            o = _rms(o, rn_ref[...]) * _silu(col(3))
            o_ref[0, rows, w + h * LIN_D:w + (h + 1) * LIN_D] = o.astype(o_ref.dtype)
        return carry

    lax.fori_loop(0, tb // LIN_CHUNK, chunk, 0)


def even_mix(proj, lb, hgrn_norm, ret_norm, tb):
    b, t, _ = proj.shape
    c = LIN_CHUNK
    log_gamma = jnp.log(1.0 - jnp.exp2(-5.0 - jnp.arange(LIN_HEADS, dtype=F32)))
    pos = jnp.arange(c, dtype=F32)
    rel = pos[:, None] - pos[None, :]
    dec = jnp.where(rel[None] >= 0, jnp.exp(jnp.maximum(rel, 0.0)[None] * log_gamma[:, None, None]), 0.0)
    ones = jnp.ones((1, 1, LIN_D), F32)
    qdec = jnp.exp((pos + 1.0)[None, :] * log_gamma[:, None])[..., None] * ones
    kdec = jnp.exp((c - 1.0 - pos)[None, :] * log_gamma[:, None])[..., None] * ones
    cdec = jnp.exp(c * log_gamma)[:, None, None] * ones
    const = lambda shape: pl.BlockSpec(shape, lambda i, j: (0,) * len(shape))
    return pl.pallas_call(
        functools.partial(_even_mix_kernel, tb=tb),
        grid=(b, t // tb),
        in_specs=[
            pl.BlockSpec((1, tb, EVEN_IN), lambda i, j: (i, j, 0)),
            const((1, LIN_HEADS * LIN_D)),
            const((1, LIN_D)),
            const((1, LIN_D)),
            const((LIN_HEADS, c, c)),
            const((LIN_HEADS, c, LIN_D)),
            const((LIN_HEADS, c, LIN_D)),
            const((LIN_HEADS, 1, LIN_D)),
        ],
        out_specs=pl.BlockSpec((1, tb, 2 * LIN_HEADS * LIN_D), lambda i, j: (i, j, 0)),
        out_shape=jax.ShapeDtypeStruct((b, t, 2 * LIN_HEADS * LIN_D), BF16),
        scratch_shapes=[pltpu.VMEM((2 * LIN_HEADS, LIN_D, LIN_D), F32), pltpu.VMEM((c, LIN_D), F32)],
        compiler_params=_cparams(("parallel", "arbitrary")),
    )(proj, lb.reshape(1, -1), hgrn_norm.reshape(1, -1), ret_norm.reshape(1, -1), dec, qdec, kdec, cdec)


def _compress_kernel(ck_ref, cv_ref, pk_ref, pv_ref, w1k_ref, w1v_ref, w2k_ref, w2v_ref, ok_ref, ov_ref):
    half = CMP_STRIDE * NSA_HD

    def one(c_ref, p_ref, w1_ref, w2_ref):
        ch = c_ref[0, 0]
        a = jnp.dot((ch + p_ref[:, :half]).astype(BF16), w1_ref[:half, :], preferred_element_type=F32)
        b = jnp.dot((ch + p_ref[:, half:]).astype(BF16), w1_ref[half:, :], preferred_element_type=F32)
        nch = a.shape[0]
        pre = a + pltpu.roll(b, nch - 1, axis=0)
        return jnp.dot(_silu(pre).astype(BF16), w2_ref[...], preferred_element_type=F32)

    ok_ref[0, 0] = one(ck_ref, pk_ref, w1k_ref, w2k_ref).astype(ok_ref.dtype)
    ov_ref[0, 0] = one(cv_ref, pv_ref, w1v_ref, w2v_ref).astype(ov_ref.dtype)


def compress(ck, cv, pos_k, pos_v, w1k, w1v, w2k, w2v):
    b, g, nch, width = ck.shape
    blk = pl.BlockSpec((1, 1, nch, width), lambda i, j: (i, j, 0, 0))
    const = lambda shape: pl.BlockSpec(shape, lambda i, j: (0,) * len(shape))
    oblk = pl.BlockSpec((1, 1, nch, NSA_HD), lambda i, j: (i, j, 0, 0))
    oshape = jax.ShapeDtypeStruct((b, g, nch, NSA_HD), BF16)
    return pl.pallas_call(
        _compress_kernel,
        grid=(b, g),
        in_specs=[blk, blk, const((1, 2 * width)), const((1, 2 * width)),
                  const((2 * width, NSA_HD)), const((2 * width, NSA_HD)),
                  const((NSA_HD, NSA_HD)), const((NSA_HD, NSA_HD))],
        out_specs=[oblk, oblk],
        out_shape=[oshape, oshape],
        compiler_params=_cparams(("parallel", "parallel")),
    )(ck, cv, pos_k.reshape(1, -1), pos_v.reshape(1, -1), w1k, w1v, w2k, w2v)


def _cmp_attn_kernel(q_ref, kc_ref, vc_ref, bias_ref, ovt_ref, oc_ref, sel_ref, flag_ref, pslc_ref, *, nq):
    n = pl.program_id(2)
    nc = kc_ref.shape[2]
    ns = ovt_ref.shape[0]
    qt = q_ref[0, 0, 0]
    row0 = pl.multiple_of((QB // CMP_STRIDE) * (nq - 1 - n), QB // CMP_STRIDE)
    tiles_needed = ((QB // CMP_STRIDE) * n + (QB - CMP_BLOCK) // CMP_STRIDE) // CT + 1

    def attend(rows):
        s = jnp.dot(kc_ref[0, 0, :rows, :], qt, preferred_element_type=F32) - bias_ref[0, pl.ds(row0, rows), :]
        m = jnp.maximum(jnp.max(s, axis=0, keepdims=True), 0.1 * NEG_INF)
        e = jnp.exp(s - m)
        l = jnp.sum(e, axis=0, keepdims=True)
        p = e * (1.0 / jnp.where(l > 0, l, 1.0))
        oc_ref[0, 0, 0] = jnp.dot(vc_ref[0, 0, :, :rows], p.astype(BF16),
                                  preferred_element_type=F32).astype(oc_ref.dtype)
        psum = p[:, 0:QB]
        for r in range(1, NSA_R):
            psum = psum + p[:, r * QB:(r + 1) * QB]
        hi = psum.astype(BF16)
        lo = (psum - hi.astype(F32)).astype(BF16)
        ovt = ovt_ref[:, :rows]
        pslc_ref[...] = (jnp.dot(ovt, hi, preferred_element_type=F32)
                         + jnp.dot(ovt, lo, preferred_element_type=F32))

    for v in range(1, nc // CT + 1):
        pl.when(tiles_needed == v)(functools.partial(attend, v * CT))
    pslc = pslc_ref[...]

    jj = lax.broadcasted_iota(jnp.int32, (ns, QB), 0)
    qblk = (n * QB + lax.broadcasted_iota(jnp.int32, (ns, QB), 1)) >> 6
    forced = (jj == 0) | (jj == qblk) | (jj == qblk - 1)
    score = jnp.where(forced, FORCE_SCORE, jnp.where(jj <= qblk, pslc, NEG_INF))
    sel = jnp.zeros((ns, QB), F32)
    for _ in range(min(SLC_TOPN, ns)):
        mx = jnp.max(score, axis=0, keepdims=True)
        idx = jnp.min(jnp.where(score == mx, jj, ns), axis=0, keepdims=True)
        pick = jj == idx
        sel = jnp.where(pick, 1.0, sel)
        score = jnp.where(pick, -jnp.inf, score)
    sel_ref[0, 0, 0] = sel
    flag_ref[0, 0, 0] = jnp.max(sel, axis=1, keepdims=True).astype(jnp.int32)


def cmp_attention(q_t, k_cmp, v_cmp_t, cmp_bias, ovt):
    b, g, nq, d, _ = q_t.shape
    nc = k_cmp.shape[2]
    ns = ovt.shape[0]
    assert nc % CT == 0 and cmp_bias.shape[1] == (QB // CMP_STRIDE) * (nq - 1) + nc
    tile = lambda shape: pl.BlockSpec((1, 1, 1) + shape, lambda i, j, n: (i, j, n, 0, 0))
    return pl.pallas_call(
        functools.partial(_cmp_attn_kernel, nq=nq),
        grid=(b, g, nq),
        in_specs=[
            tile((d, HQ)),
            pl.BlockSpec((1, 1, nc, d), lambda i, j, n: (i, j, 0, 0)),
            pl.BlockSpec((1, 1, d, nc), lambda i, j, n: (i, j, 0, 0)),
            pl.BlockSpec((1,) + cmp_bias.shape[1:], lambda i, j, n: (j, 0, 0)),
            pl.BlockSpec((ns, nc), lambda i, j, n: (0, 0)),
        ],
        out_specs=[tile((d, HQ)), tile((ns, QB)), tile((ns, 1))],
        out_shape=[
            jax.ShapeDtypeStruct((b, g, nq, d, HQ), BF16),
            jax.ShapeDtypeStruct((b, g, nq, ns, QB), F32),
            jax.ShapeDtypeStruct((b, g, nq, ns, 1), jnp.int32),
        ],
        scratch_shapes=[pltpu.VMEM((ns, QB), F32)],
        compiler_params=_cparams(("parallel", "parallel", "arbitrary")),
    )(q_t, k_cmp, v_cmp_t, cmp_bias, ovt)


def _flash_step(qt, tiles, state):
    half = SLC_BLOCK
    scores, tile_max = [], None
    for kb, _, bias, off, chosen in tiles:
        s = jnp.dot(kb, qt, preferred_element_type=F32) - bias
        if chosen is None:
            mx = jnp.max(s, axis=0, keepdims=True)
        else:
            mx = jnp.maximum(jnp.where(chosen[0] > 0, jnp.max(s[:half], axis=0, keepdims=True), NEG_INF),
                             jnp.where(chosen[1] > 0, jnp.max(s[half:], axis=0, keepdims=True), NEG_INF))
        scores.append(s)
        tile_max = mx - off if tile_max is None else jnp.maximum(tile_max, mx - off)
    if state is None:
        m_new, l, acc = tile_max, 0.0, 0.0
    else:
        m_old, l_old, acc_old = state
        m_new = jnp.maximum(m_old, tile_max)
        alpha = jnp.exp(m_old - m_new)
        l, acc = alpha * l_old, alpha * acc_old
    for s, (_, vtb, _, off, chosen) in zip(scores, tiles):
        c0 = c1 = m_new + off
        if chosen is not None:
            c0 = jnp.where(chosen[0] > 0, c0, -NEG_INF)
            c1 = jnp.where(chosen[1] > 0, c1, -NEG_INF)
        p0 = jnp.exp(s[:half] - c0)
        p1 = jnp.exp(s[half:] - c1)
        l = l + (jnp.sum(p0, axis=0, keepdims=True) + jnp.sum(p1, axis=0, keepdims=True))
        acc = acc + jnp.dot(vtb, jnp.concatenate([p0, p1], axis=0).astype(BF16), preferred_element_type=F32)
    return m_new, l, acc


BIAS_MID, BIAS_DIAG, BIAS_FIRST = 0, 1, 2


SEL_GROUP = 2


def _sel_win_kernel(flags_ref, q_ref, ks_ref, vs_ref, kw_ref, vw_ref, sel_ref, oc_ref, gl_ref, slope_ref,
                    bias_ref, o_ref, ms_ref, ls_ref, accs_ref, list_ref, *, nq):
    bi, gi, n = pl.program_id(0), pl.program_id(1), pl.program_id(2)
    ns = sel_ref.shape[3]
    qt = q_ref[0, 0, 0]
    slope = slope_ref[0]
    rows_of = lambda p: pl.ds(pl.multiple_of(p * KT, KT), KT)

    nwin = WINDOW // KT
    tiles = []
    for j in range(nwin + 1):
        p = n - nwin + j
        pc = jnp.maximum(p, 0)
        off = slope * ((nwin - j) * KT) + jnp.where(p >= 0, 0.0, -NEG_INF)
        kind = BIAS_FIRST if j == 0 else (BIAS_DIAG if j == nwin else BIAS_MID)
        tiles.append((kw_ref[0, 0, rows_of(pc), :], vw_ref[0, 0, pc], bias_ref[0, kind], off, None))
    _, l_w, acc_w = _flash_step(qt, tiles, None)
    o_w = acc_w * (1.0 / l_w)

    fbase = ((bi * NSA_G + gi) * nq + n) * (ns // 2)

    def scan(p, cnt):
        hit = flags_ref[fbase + p] != 0

        @pl.when(hit)
        def _():
            list_ref[cnt] = p
        return cnt + hit.astype(jnp.int32)

    cnt = lax.fori_loop(0, n, scan, 0)
    list_ref[cnt] = n
    cnt = cnt + 1
    ms_ref[...] = jnp.full(ms_ref.shape, NEG_INF, F32)
    ls_ref[...] = jnp.zeros(ls_ref.shape, F32)
    accs_ref[...] = jnp.zeros(accs_ref.shape, F32)

    def sel_body(it, carry):
        tiles = []
        for u in range(SEL_GROUP):
            e = it * SEL_GROUP + u
            live = e < cnt
            p = list_ref[jnp.minimum(e, cnt - 1)]
            keep = jnp.where(live, 1.0, 0.0)
            chosen = [keep * jnp.concatenate([sel_ref[0, 0, 0, pl.ds(2 * p + i, 1), :]] * NSA_R, axis=1)
                      for i in (0, 1)]
            kind = jnp.where(p == n, BIAS_DIAG, BIAS_MID)
            off = slope * ((n - p) * KT).astype(F32)
            tiles.append((ks_ref[0, 0, rows_of(p), :], vs_ref[0, 0, p], bias_ref[0, kind], off, chosen))
        m, l, acc = _flash_step(qt, tiles, (ms_ref[...], ls_ref[...], accs_ref[...]))
        ms_ref[...] = m
        ls_ref[...] = l
        accs_ref[...] = acc
        return carry

    lax.fori_loop(0, (cnt + SEL_GROUP - 1) // SEL_GROUP, sel_body, 0)
    o_s = accs_ref[...] * (1.0 / ls_ref[...])

    gates = jax.nn.sigmoid(gl_ref[0, 0, 0])
    o = gates[0:1] * oc_ref[0, 0, 0].astype(F32) + gates[1:2] * o_s + gates[2:3] * o_w
    o_ref[0, 0, 0] = o.astype(o_ref.dtype)


def sel_win_attention(pair_flags, q_t, ks, vs_t, kw, vw_t, sel, o_c, gl, slope_row, bias):
    b, g, nq, d, _ = q_t.shape
    t = ks.shape[2]
    ns = sel.shape[3]
    tile = lambda shape: pl.BlockSpec((1, 1, 1) + shape, lambda i, j, n, f: (i, j, n, 0, 0))
    keys = pl.BlockSpec((1, 1, t, d), lambda i, j, n, f: (i, j, 0, 0))
    vals = pl.BlockSpec((1, 1, t // KT, d, KT), lambda i, j, n, f: (i, j, 0, 0, 0))
    grid_spec = pltpu.PrefetchScalarGridSpec(
        num_scalar_prefetch=1,
        grid=(b, g, nq),
        in_specs=[
            tile((d, HQ)), keys, vals, keys, vals,
            tile((ns, QB)), tile((d, HQ)), tile((3, HQ)),
            pl.BlockSpec((1, 1, HQ), lambda i, j, n, f: (j, 0, 0)),
            pl.BlockSpec((1, 3, KT, HQ), lambda i, j, n, f: (j, 0, 0, 0)),
        ],
        out_specs=tile((d, HQ)),
        scratch_shapes=[pltpu.VMEM((1, HQ), F32), pltpu.VMEM((1, HQ), F32), pltpu.VMEM((d, HQ), F32),
                        pltpu.SMEM((nq,), jnp.int32)],
    )
    return pl.pallas_call(
        functools.partial(_sel_win_kernel, nq=nq),
        grid_spec=grid_spec,
        out_shape=jax.ShapeDtypeStruct((b, g, nq, d, HQ), BF16),
        compiler_params=_cparams(("parallel", "parallel", "arbitrary")),
    )(pair_flags, q_t, ks, vs_t, kw, vw_t, sel, o_c, gl, slope_row, bias)


def _row_tile(m):
    return 512 if m % 512 == 0 else m


def even_layer_mixer(h, norm_g, w_in, lb, hgrn_norm, ret_norm, w_out):
    b, t, dm = h.shape
    x2 = h.reshape(b * t, dm)
    tm = _row_tile(b * t)
    proj = norm_matmul(x2, norm_g, w_in.astype(BF16), tm, 1024)
    o = even_mix(proj.reshape(b, t, EVEN_IN), lb, hgrn_norm, ret_norm, min(t, 256))
    return matmul_residual(o.reshape(b * t, -1), w_out.astype(BF16), x2, tm).reshape(b, t, dm)


def odd_layer_mixer(h, norm_g, w_in, cmp_pos_k, cmp_pos_v, cmp_w1_k, cmp_w2_k, cmp_w1_v, cmp_w2_v, w_out):
    b, t, dm = h.shape
    g_, r_, d = NSA_G, NSA_R, NSA_HD
    nq = t // QB
    ns = t // SLC_BLOCK
    nch = t // CMP_STRIDE
    x2 = h.reshape(b * t, dm)
    tm = _row_tile(b * t)
    w_pad = jnp.pad(w_in, ((0, 0), (0, ODD_IN_PAD - ODD_IN))).astype(BF16)
    proj = norm_matmul(x2, norm_g, w_pad, tm, ODD_IN_PAD // 3)
    kvw = g_ * d
    o0 = NSA_HEADS * d
    q = proj[:, :o0] * (d ** -0.5)
    part = lambda i: proj[:, o0 + i * kvw:o0 + (i + 1) * kvw]
    gl = proj[:, o0 + 6 * kvw:o0 + 6 * kvw + 3 * NSA_HEADS]

    q_t = q.reshape(b, nq, QB, g_, r_, d).transpose(0, 3, 1, 5, 4, 2).reshape(b, g_, nq, d, HQ).astype(BF16)
    gl_t = gl.reshape(b, nq, QB, g_, r_, 3).transpose(0, 3, 1, 5, 4, 2).reshape(b, g_, nq, 3, HQ)
    keys = lambda a: a.reshape(b, t, g_, d).transpose(0, 2, 1, 3).astype(BF16)
    vals = lambda a: a.reshape(b, t // KT, KT, g_, d).transpose(0, 3, 1, 4, 2).astype(BF16)
    chunks = lambda a: a.reshape(b, nch, CMP_STRIDE, g_, d).transpose(0, 3, 1, 2, 4).reshape(b, g_, nch, CMP_STRIDE * d)
    ks, vs_t, kw, vw_t = keys(part(2)), vals(part(3)), keys(part(4)), vals(part(5))

    k_cmp, v_cmp = compress(chunks(part(0)), chunks(part(1)), cmp_pos_k, cmp_pos_v,
                            cmp_w1_k.astype(BF16), cmp_w1_v.astype(BF16),
                            cmp_w2_k.astype(BF16), cmp_w2_v.astype(BF16))
    v_cmp_t = v_cmp.transpose(0, 1, 3, 2)

    slopes = jnp.exp2(-8.0 * jnp.arange(1, NSA_HEADS + 1, dtype=F32) / NSA_HEADS).reshape(g_, r_)
    slope_row = jnp.repeat(slopes, QB, axis=1).reshape(g_, 1, HQ)
    tq = np.tile(np.arange(QB), r_)[None, :]
    c_rel = np.arange(-(QB // CMP_STRIDE) * (nq - 1), nch)[:, None]
    dist_rel = tq - (c_rel * CMP_STRIDE + CMP_BLOCK - 1)
    cmp_bias = (slope_row * jnp.asarray(dist_rel, F32)[None]
                + jnp.asarray(np.where(dist_rel >= 0, 0.0, -NEG_INF), F32)[None])
    rel = tq - np.arange(KT)[:, None]
    mid = slope_row * jnp.asarray(rel, F32)[None]
    bias = jnp.stack([mid, mid + jnp.asarray(np.where(rel >= 0, 0.0, -NEG_INF), F32),
                      mid + jnp.asarray(np.where(rel < 0, 0.0, -NEG_INF), F32)], axis=1)
    c0 = np.arange(nch) * CMP_STRIDE
    s0 = np.arange(ns) * SLC_BLOCK
    ov = np.maximum(np.minimum(c0[:, None] + CMP_BLOCK, s0[None, :] + SLC_BLOCK)
                    - np.maximum(c0[:, None], s0[None, :]), 0) / CMP_BLOCK
    ov[nch - (CMP_BLOCK // CMP_STRIDE - 1):] = 0.0
    ovt = jnp.asarray(ov.T, BF16)

    o_c, sel, flags = cmp_attention(q_t, k_cmp, v_cmp_t, cmp_bias, ovt)
    pair_flags = flags.reshape(b, g_, nq, ns // 2, 2).max(axis=-1).reshape(-1)
    o = sel_win_attention(pair_flags, q_t, ks, vs_t, kw, vw_t, sel, o_c, gl_t, slope_row, bias)
    o = o.reshape(b, g_, nq, d, r_, QB).transpose(0, 2, 5, 1, 4, 3).reshape(b * t, NSA_HEADS * d)
    return matmul_residual(o, w_out.astype(BF16), x2, tm).reshape(b, t, dm)


def ffn_layer(h, norm_g, w_gate_up, w_down, final_g):
    b, t, dm = h.shape
    out = ffn(h.reshape(b * t, dm), norm_g, w_gate_up.astype(BF16), w_down.astype(BF16), final_g,
              _row_tile(b * t), 256)
    return out.reshape(b, t, dm)


def kernel(x, mix_norm, ffn_norm, final_norm, even_w_in, hgrn_lower_bounds, hgrn_out_norm, ret_out_norm,
           even_w_out, odd_w_in, cmp_pos_k, cmp_pos_v, cmp_w1_k, cmp_w2_k, cmp_w1_v, cmp_w2_v, odd_w_out,
           ffn_w_gate_up, ffn_w_down):
    depth = mix_norm.shape[0]
    lb_all = jnp.cumsum(jax.nn.softmax(hgrn_lower_bounds.astype(F32), axis=0), axis=0)
    h = x
    for layer in range(depth):
        if layer % 2 == 0:
            e = layer // 2
            h = even_layer_mixer(h, mix_norm[layer], even_w_in[e], lb_all[e], hgrn_out_norm[e],
                                 ret_out_norm[e], even_w_out[e])
        else:
            o = layer // 2
            h = odd_layer_mixer(h, mix_norm[layer], odd_w_in[o], cmp_pos_k[o], cmp_pos_v[o], cmp_w1_k[o],
                                cmp_w2_k[o], cmp_w1_v[o], cmp_w2_v[o], odd_w_out[o])
        h = ffn_layer(h, ffn_norm[layer], ffn_w_gate_up[layer], ffn_w_down[layer],
                      final_norm if layer == depth - 1 else None)
    return h
```

```python
import functools

import numpy as np
import jax
import jax.numpy as jnp
from jax import lax
from jax.experimental import pallas as pl
from jax.experimental.pallas import tpu as pltpu

F32 = jnp.float32
BF16 = jnp.bfloat16

D_MODEL = 1024
RMS_EPS = 1e-6
NEG_INF = -1e30
FORCE_SCORE = 1e9

LIN_HEADS = 4
LIN_D = 128
LIN_CHUNK = 64
SUB = 16
EVEN_IN = 8 * LIN_HEADS * LIN_D

NSA_HD = 64
NSA_HEADS = 16
NSA_G = 2
NSA_R = NSA_HEADS // NSA_G
CMP_BLOCK = 32
CMP_STRIDE = 16
SLC_BLOCK = 64
SLC_TOPN = 16
WINDOW = 512
QB = 128
KT = 128
CT = 128
HQ = NSA_R * QB
LOG2E = 1.4426950408889634

FFN_HIDDEN = 2816

VMEM_LIMIT = 56 * 1024 * 1024

_NT = (((1,), (1,)), ((), ()))
_TN = (((0,), (0,)), ((), ()))


def _cparams(sem):
    return pltpu.CompilerParams(dimension_semantics=sem, vmem_limit_bytes=VMEM_LIMIT)


def _rms(x, g):
    ms = jnp.mean(x * x, axis=-1, keepdims=True)
    return x * lax.rsqrt(ms + RMS_EPS) * g


def _silu(x):
    return x * jax.nn.sigmoid(x)


def _norm_matmul_kernel(x_ref, g_ref, w_ref, o_ref, *, tn):
    xn = _rms(x_ref[...], g_ref[...]).astype(BF16)
    for c in range(w_ref.shape[1] // tn):
        cols = slice(c * tn, (c + 1) * tn)
        o_ref[:, cols] = jnp.dot(xn, w_ref[:, cols], preferred_element_type=F32).astype(o_ref.dtype)


def norm_matmul(x, g, w, tm, tn, out_dtype):
    m, k = x.shape
    n = w.shape[1]
    return pl.pallas_call(
        functools.partial(_norm_matmul_kernel, tn=tn),
        grid=(m // tm,),
        in_specs=[
            pl.BlockSpec((tm, k), lambda i: (i, 0)),
            pl.BlockSpec((1, k), lambda i: (0, 0)),
            pl.BlockSpec((k, n), lambda i: (0, 0)),
        ],
        out_specs=pl.BlockSpec((tm, n), lambda i: (i, 0)),
        out_shape=jax.ShapeDtypeStruct((m, n), out_dtype),
        compiler_params=_cparams(("parallel",)),
    )(x, g.reshape(1, k), w)


def _matmul_res_kernel(a_ref, w_ref, r_ref, o_ref):
    o_ref[...] = r_ref[...] + jnp.dot(a_ref[...], w_ref[...], preferred_element_type=F32)


def matmul_residual(a, w, res, tm):
    m, k = a.shape
    n = w.shape[1]
    return pl.pallas_call(
        _matmul_res_kernel,
        grid=(m // tm,),
        in_specs=[
            pl.BlockSpec((tm, k), lambda i: (i, 0)),
            pl.BlockSpec((k, n), lambda i: (0, 0)),
            pl.BlockSpec((tm, n), lambda i: (i, 0)),
        ],
        out_specs=pl.BlockSpec((tm, n), lambda i: (i, 0)),
        out_shape=jax.ShapeDtypeStruct((m, n), F32),
        compiler_params=_cparams(("parallel",)),
    )(a, w, res)


def _ffn_kernel(x_ref, g_ref, wgu_ref, wd_ref, fg_ref, o_ref, xn_ref, acc_ref, *, final_norm):
    j = pl.program_id(1)
    th = wd_ref.shape[0]

    @pl.when(j == 0)
    def _():
        xn_ref[...] = _rms(x_ref[...], g_ref[...]).astype(BF16)
        acc_ref[...] = jnp.zeros_like(acc_ref)

    xn = xn_ref[...]
    gate = jnp.dot(xn, wgu_ref[0, :, :th], preferred_element_type=F32)
    up = jnp.dot(xn, wgu_ref[0, :, th:], preferred_element_type=F32)
    a = (_silu(gate) * up).astype(BF16)
    acc_ref[...] += jnp.dot(a, wd_ref[...], preferred_element_type=F32)

    @pl.when(j == pl.num_programs(1) - 1)
    def _():
        h = x_ref[...] + acc_ref[...]
        if final_norm:
            h = _rms(h, fg_ref[...])
        o_ref[...] = h


def ffn(x, g, w_gate_up, w_down, final_g, tm, th):
    m, k = x.shape
    hid = w_down.shape[0]
    nj = hid // th
    final_norm = final_g is not None
    fg = final_g if final_norm else g
    w_gu = w_gate_up.reshape(k, 2, nj, th).transpose(2, 0, 1, 3).reshape(nj, k, 2 * th)
    return pl.pallas_call(
        functools.partial(_ffn_kernel, final_norm=final_norm),
        grid=(m // tm, nj),
        in_specs=[
            pl.BlockSpec((tm, k), lambda i, j: (i, 0)),
            pl.BlockSpec((1, k), lambda i, j: (0, 0)),
            pl.BlockSpec((1, k, 2 * th), lambda i, j: (j, 0, 0)),
            pl.BlockSpec((th, k), lambda i, j: (j, 0)),
            pl.BlockSpec((1, k), lambda i, j: (0, 0)),
        ],
        out_specs=pl.BlockSpec((tm, k), lambda i, j: (i, 0)),
        out_shape=jax.ShapeDtypeStruct((m, k), F32),
        scratch_shapes=[pltpu.VMEM((tm, k), BF16), pltpu.VMEM((tm, k), F32)],
        compiler_params=_cparams(("parallel", "arbitrary")),
    )(x, g.reshape(1, k), w_gu, w_down, fg.reshape(1, k))


def _cumsum_rows(g):
    c = g.shape[0]
    row = lax.broadcasted_iota(jnp.int32, g.shape, 0)
    s = 1
    while s < c:
        g = g + jnp.where(row >= s, pltpu.roll(g, s, axis=0), 0.0)
        s *= 2
    return g


def _hgrn_chunk(hq, hf, hi, lb, st_ref, h, o_scr):
    c = LIN_CHUNK
    f = lb + (1.0 - lb) * jax.nn.sigmoid(hf)
    k = 1.0 - f
    q = _silu(hq)
    v = hi
    cum = _cumsum_rows(jnp.log(f))
    st = st_ref[h]
    o_inter = lax.dot_general((q * jnp.exp(cum)).astype(BF16), st.astype(BF16), _NT,
                              preferred_element_type=F32)
    jrow = lax.broadcasted_iota(jnp.int32, (SUB, LIN_D), 0)
    vb = v.astype(BF16)
    for a in range(c // SUB):
        lo = a * SUB
        cum_a = cum[lo:lo + SUB]
        q_a = q[lo:lo + SUB]
        k_a = k[lo:lo + SUB]
        v_a = v[lo:lo + SUB]
        for i in range(SUB):
            diff = cum_a[i:i + 1, :] - cum_a
            dec = jnp.exp(jnp.where(jrow <= i, diff, NEG_INF))
            w = (q_a[i:i + 1, :] * k_a) * dec
            s = jnp.sum(w, axis=-1, keepdims=True)
            o_scr[lo + i:lo + i + 1, :] = jnp.sum(s * v_a, axis=0, keepdims=True)
        if a > 0:
            ref = cum[lo - 1:lo, :]
            qd = (q_a * jnp.exp(cum_a - ref)).astype(BF16)
            kd = (k[:lo] * jnp.exp(ref - cum[:lo])).astype(BF16)
            attn = lax.dot_general(qd, kd, _NT, preferred_element_type=F32)
            o_scr[lo:lo + SUB, :] += jnp.dot(attn.astype(BF16), vb[:lo], preferred_element_type=F32)
    last = cum[c - 1:c, :]
    kd = (k * jnp.exp(last - cum)).astype(BF16)
    st_ref[h] = st * jnp.exp(last) + lax.dot_general(vb, kd, _TN, preferred_element_type=F32)
    return o_scr[...] + o_inter


def _ret_chunk(rq, rk, rv, dec, qdec, kdec, cdec, st_ref, h):
    k = rk * (LIN_D ** -0.5)
    st = st_ref[h]
    qb = rq.astype(BF16)
    vb = rv.astype(BF16)
    attn = lax.dot_general(qb, k.astype(BF16), _NT, preferred_element_type=F32) * dec
    o = jnp.dot(attn.astype(BF16), vb, preferred_element_type=F32)
    o = o + lax.dot_general((rq * qdec).astype(BF16), st.astype(BF16), _NT, preferred_element_type=F32)
    st_ref[h] = st * cdec + lax.dot_general(vb, (k * kdec).astype(BF16), _TN, preferred_element_type=F32)
    return o


def _even_mix_kernel(p_ref, lb_ref, hn_ref, rn_ref, dec_ref, qdec_ref, kdec_ref, cdec_ref,
                     o_ref, st_ref, o_scr, *, tb):
    @pl.when(pl.program_id(1) == 0)
    def _():
        st_ref[...] = jnp.zeros_like(st_ref)

    w = LIN_HEADS * LIN_D

    def chunk(ci, carry):
        rows = pl.ds(pl.multiple_of(ci * LIN_CHUNK, LIN_CHUNK), LIN_CHUNK)
        for h in range(LIN_HEADS):
            col = lambda part: p_ref[0, rows, part * w + h * LIN_D:part * w + (h + 1) * LIN_D].astype(F32)
            o = _hgrn_chunk(col(0), col(1), col(2), lb_ref[:, h * LIN_D:(h + 1) * LIN_D], st_ref, h, o_scr)
            o = _rms(o, hn_ref[...]) * _silu(col(3))
            o_ref[0, rows, h * LIN_D:(h + 1) * LIN_D] = o.astype(o_ref.dtype)
        for h in range(LIN_HEADS):
            col = lambda part: p_ref[0, rows, (4 + part) * w + h * LIN_D:(4 + part) * w + (h + 1) * LIN_D].astype(F32)
            o = _ret_chunk(col(0), col(1), col(2), dec_ref[h], qdec_ref[h], kdec_ref[h], cdec_ref[h],
                           st_ref, LIN_HEADS + h)
            o = _rms(o, rn_ref[...]) * _silu(col(3))
            o_ref[0, rows, w + h * LIN_D:w + (h + 1) * LIN_D] = o.astype(o_ref.dtype)
        return carry

    lax.fori_loop(0, tb // LIN_CHUNK, chunk, 0)


def even_mix(proj, lb, hgrn_norm, ret_norm, tb):
    b, t, _ = proj.shape
    c = LIN_CHUNK
    log_gamma = jnp.log(1.0 - jnp.exp2(-5.0 - jnp.arange(LIN_HEADS, dtype=F32)))
    pos = jnp.arange(c, dtype=F32)
    rel = pos[:, None] - pos[None, :]
    dec = jnp.where(rel[None] >= 0, jnp.exp(jnp.maximum(rel, 0.0)[None] * log_gamma[:, None, None]), 0.0)
    ones = jnp.ones((1, 1, LIN_D), F32)
    qdec = jnp.exp((pos + 1.0)[None, :] * log_gamma[:, None])[..., None] * ones
    kdec = jnp.exp((c - 1.0 - pos)[None, :] * log_gamma[:, None])[..., None] * ones
    cdec = jnp.exp(c * log_gamma)[:, None, None] * ones
    const = lambda shape: pl.BlockSpec(shape, lambda i, j: (0,) * len(shape))
    return pl.pallas_call(
        functools.partial(_even_mix_kernel, tb=tb),
        grid=(b, t // tb),
        in_specs=[
            pl.BlockSpec((1, tb, EVEN_IN), lambda i, j: (i, j, 0)),
            const((1, LIN_HEADS * LIN_D)),
            const((1, LIN_D)),
            const((1, LIN_D)),
            const((LIN_HEADS, c, c)),
            const((LIN_HEADS, c, LIN_D)),
            const((LIN_HEADS, c, LIN_D)),
            const((LIN_HEADS, 1, LIN_D)),
        ],
        out_specs=pl.BlockSpec((1, tb, 2 * LIN_HEADS * LIN_D), lambda i, j: (i, j, 0)),
        out_shape=jax.ShapeDtypeStruct((b, t, 2 * LIN_HEADS * LIN_D), BF16),
        scratch_shapes=[pltpu.VMEM((2 * LIN_HEADS, LIN_D, LIN_D), F32), pltpu.VMEM((c, LIN_D), F32)],
        compiler_params=_cparams(("parallel", "arbitrary")),
    )(proj, lb.reshape(1, -1), hgrn_norm.reshape(1, -1), ret_norm.reshape(1, -1), dec, qdec, kdec, cdec)


def _odd_proj_kernel(x_ref, g_ref, wq_ref, wn_ref, wv_ref, wg_ref,
                     q_ref, kc_ref, vc_ref, ks_ref, kw_ref, vs_ref, vw_ref, gl_ref, *, q_scale):
    g_, r_, d = NSA_G, NSA_R, NSA_HD
    xn = _rms(x_ref[0], g_ref[...]).astype(BF16)
    tm = xn.shape[0]
    qt = (lax.dot_general(wq_ref[...], xn, _NT, preferred_element_type=F32) * q_scale).astype(BF16)
    for g in range(g_):
        for r in range(r_):
            rows = slice((g * r_ + r) * d, (g * r_ + r + 1) * d)
            for j in range(tm // QB):
                q_ref[0, g, j, :, r * QB:(r + 1) * QB] = qt[rows, j * QB:(j + 1) * QB]
    nat = jnp.dot(xn, wn_ref[...], preferred_element_type=F32)
    for i, ref in enumerate((kc_ref, vc_ref, ks_ref, kw_ref)):
        for g in range(g_):
            ref[0, g] = nat[:, (i * g_ + g) * d:(i * g_ + g + 1) * d].astype(ref.dtype)
    vt = lax.dot_general(wv_ref[...], xn, _NT, preferred_element_type=F32).astype(BF16)
    for i, ref in enumerate((vs_ref, vw_ref)):
        for g in range(g_):
            for j in range(tm // KT):
                ref[0, g, j] = vt[(i * g_ + g) * d:(i * g_ + g + 1) * d, j * KT:(j + 1) * KT]
    glt = lax.dot_general(wg_ref[...], xn, _NT, preferred_element_type=F32)
    for g in range(g_):
        for br in range(3):
            for r in range(r_):
                row = (g * 3 + br) * r_ + r
                for j in range(tm // QB):
                    gl_ref[0, g, j, br:br + 1, r * QB:(r + 1) * QB] = glt[row:row + 1, j * QB:(j + 1) * QB]


def odd_project(h, norm_g, w_in, tm, q_scale):
    b, t, dm = h.shape
    g_, r_, d = NSA_G, NSA_R, NSA_HD
    kvw = g_ * d
    o0 = NSA_HEADS * d
    part = lambda i: w_in[:, o0 + i * kvw:o0 + (i + 1) * kvw]
    wq = w_in[:, :o0].T.astype(BF16)
    wn = jnp.concatenate([part(0), part(1), part(2), part(4)], axis=1).astype(BF16)
    wv = jnp.concatenate([part(3), part(5)], axis=1).T.astype(BF16)
    wg = w_in[:, o0 + 6 * kvw:o0 + 6 * kvw + 3 * NSA_HEADS]
    wg = wg.reshape(dm, g_, r_, 3).transpose(1, 3, 2, 0).reshape(g_ * 3 * r_, dm).astype(BF16)
    const = lambda a: pl.BlockSpec(a.shape, lambda i, j: (0,) * a.ndim)
    nqt, nkt = tm // QB, tm // KT
    tok = pl.BlockSpec((1, g_, tm, d), lambda i, j: (i, 0, j, 0))
    tok_shape = lambda dt: jax.ShapeDtypeStruct((b, g_, t, d), dt)
    vt_spec = pl.BlockSpec((1, g_, nkt, d, KT), lambda i, j: (i, 0, j, 0, 0))
    vt_shape = jax.ShapeDtypeStruct((b, g_, t // KT, d, KT), BF16)
    return pl.pallas_call(
        functools.partial(_odd_proj_kernel, q_scale=q_scale),
        grid=(b, t // tm),
        in_specs=[pl.BlockSpec((1, tm, dm), lambda i, j: (i, j, 0)), pl.BlockSpec((1, dm), lambda i, j: (0, 0)),
                  const(wq), const(wn), const(wv), const(wg)],
        out_specs=[pl.BlockSpec((1, g_, nqt, d, HQ), lambda i, j: (i, 0, j, 0, 0)),
                   tok, tok, tok, tok, vt_spec, vt_spec,
                   pl.BlockSpec((1, g_, nqt, 3, HQ), lambda i, j: (i, 0, j, 0, 0))],
        out_shape=[jax.ShapeDtypeStruct((b, g_, t // QB, d, HQ), BF16),
                   tok_shape(F32), tok_shape(F32), tok_shape(BF16), tok_shape(BF16), vt_shape, vt_shape,
                   jax.ShapeDtypeStruct((b, g_, t // QB, 3, HQ), F32)],
        compiler_params=_cparams(("parallel", "parallel")),
    )(h, norm_g.reshape(1, dm), wq, wn, wv, wg)


def _compress_kernel(ck_ref, cv_ref, pk_ref, pv_ref, w1k_ref, w1v_ref, w2k_ref, w2v_ref, ok_ref, ov_ref):
    half = CMP_STRIDE * NSA_HD

    def one(c_ref, p_ref, w1_ref, w2_ref):
        ch = c_ref[0, 0]
        a = jnp.dot((ch + p_ref[:, :half]).astype(BF16), w1_ref[:half, :], preferred_element_type=F32)
        b = jnp.dot((ch + p_ref[:, half:]).astype(BF16), w1_ref[half:, :], preferred_element_type=F32)
        nch = a.shape[0]
        pre = a + pltpu.roll(b, nch - 1, axis=0)
        return jnp.dot(_silu(pre).astype(BF16), w2_ref[...], preferred_element_type=F32)

    ok_ref[0, 0] = one(ck_ref, pk_ref, w1k_ref, w2k_ref).astype(ok_ref.dtype)
    ov_ref[0, 0] = one(cv_ref, pv_ref, w1v_ref, w2v_ref).astype(ov_ref.dtype)


def compress(ck, cv, pos_k, pos_v, w1k, w1v, w2k, w2v):
    b, g, nch, width = ck.shape
    blk = pl.BlockSpec((1, 1, nch, width), lambda i, j: (i, j, 0, 0))
    const = lambda shape: pl.BlockSpec(shape, lambda i, j: (0,) * len(shape))
    oblk = pl.BlockSpec((1, 1, nch, NSA_HD), lambda i, j: (i, j, 0, 0))
    oshape = jax.ShapeDtypeStruct((b, g, nch, NSA_HD), BF16)
    return pl.pallas_call(
        _compress_kernel,
        grid=(b, g),
        in_specs=[blk, blk, const((1, 2 * width)), const((1, 2 * width)),
                  const((2 * width, NSA_HD)), const((2 * width, NSA_HD)),
                  const((NSA_HD, NSA_HD)), const((NSA_HD, NSA_HD))],
        out_specs=[oblk, oblk],
        out_shape=[oshape, oshape],
        compiler_params=_cparams(("parallel", "parallel")),
    )(ck, cv, pos_k.reshape(1, -1), pos_v.reshape(1, -1), w1k, w1v, w2k, w2v)


def _cmp_attn_kernel(q_ref, kc_ref, vc_ref, bias_ref, ovt_ref, oc_ref, sel_ref, flag_ref, pslc_ref, *, nq):
    n = pl.program_id(2)
    nc = kc_ref.shape[2]
    ns = ovt_ref.shape[0]
    qt = q_ref[0, 0, 0]
    row0 = pl.multiple_of((QB // CMP_STRIDE) * (nq - 1 - n), QB // CMP_STRIDE)
    tiles_needed = ((QB // CMP_STRIDE) * n + (QB - CMP_BLOCK) // CMP_STRIDE) // CT + 1

    def attend(rows):
        s = jnp.dot(kc_ref[0, 0, :rows, :], qt, preferred_element_type=F32) - bias_ref[0, pl.ds(row0, rows), :]
        m = jnp.maximum(jnp.max(s, axis=0, keepdims=True), 0.1 * NEG_INF)
        e = jnp.exp2(s - m)
        l = jnp.sum(e, axis=0, keepdims=True)
        p = e * (1.0 / jnp.where(l > 0, l, 1.0))
        oc_ref[0, 0, 0] = jnp.dot(vc_ref[0, 0, :, :rows], p.astype(BF16),
                                  preferred_element_type=F32).astype(oc_ref.dtype)
        psum = p[:, 0:QB]
        for r in range(1, NSA_R):
            psum = psum + p[:, r * QB:(r + 1) * QB]
        hi = psum.astype(BF16)
        lo = (psum - hi.astype(F32)).astype(BF16)
        ovt = ovt_ref[:, :rows]
        pslc_ref[...] = (jnp.dot(ovt, hi, preferred_element_type=F32)
                         + jnp.dot(ovt, lo, preferred_element_type=F32))

    for v in range(1, nc // CT + 1):
        pl.when(tiles_needed == v)(functools.partial(attend, v * CT))
    pslc = pslc_ref[...]

    jj = lax.broadcasted_iota(jnp.int32, (ns, QB), 0)
    qblk = (n * QB + lax.broadcasted_iota(jnp.int32, (ns, QB), 1)) >> 6
    forced = (jj == 0) | (jj == qblk) | (jj == qblk - 1)
    score = jnp.where(forced, FORCE_SCORE, jnp.where(jj <= qblk, pslc, NEG_INF))
    sel = jnp.zeros((ns, QB), F32)
    for _ in range(min(SLC_TOPN, ns)):
        mx = jnp.max(score, axis=0, keepdims=True)
        idx = jnp.min(jnp.where(score == mx, jj, ns), axis=0, keepdims=True)
        pick = jj == idx
        sel = jnp.where(pick, 1.0, sel)
        score = jnp.where(pick, -jnp.inf, score)
    sel_ref[0, 0, 0] = sel
    flag_ref[0, 0, 0] = jnp.max(sel, axis=1, keepdims=True).astype(jnp.int32)


def cmp_attention(q_t, k_cmp, v_cmp_t, cmp_bias, ovt):
    b, g, nq, d, _ = q_t.shape
    nc = k_cmp.shape[2]
    ns = ovt.shape[0]
    assert nc % CT == 0 and cmp_bias.shape[1] == (QB // CMP_STRIDE) * (nq - 1) + nc
    tile = lambda shape: pl.BlockSpec((1, 1, 1) + shape, lambda i, j, n: (i, j, n, 0, 0))
    return pl.pallas_call(
        functools.partial(_cmp_attn_kernel, nq=nq),
        grid=(b, g, nq),
        in_specs=[
            tile((d, HQ)),
            pl.BlockSpec((1, 1, nc, d), lambda i, j, n: (i, j, 0, 0)),
            pl.BlockSpec((1, 1, d, nc), lambda i, j, n: (i, j, 0, 0)),
            pl.BlockSpec((1,) + cmp_bias.shape[1:], lambda i, j, n: (j, 0, 0)),
            pl.BlockSpec((ns, nc), lambda i, j, n: (0, 0)),
        ],
        out_specs=[tile((d, HQ)), tile((ns, QB)), tile((ns, 1))],
        out_shape=[
            jax.ShapeDtypeStruct((b, g, nq, d, HQ), BF16),
            jax.ShapeDtypeStruct((b, g, nq, ns, QB), F32),
            jax.ShapeDtypeStruct((b, g, nq, ns, 1), jnp.int32),
        ],
        scratch_shapes=[pltpu.VMEM((ns, QB), F32)],
        compiler_params=_cparams(("parallel", "parallel", "arbitrary")),
    )(q_t, k_cmp, v_cmp_t, cmp_bias, ovt)


def _flash_step(qt, tiles, state):
    half = SLC_BLOCK
    scores, tile_max = [], None
    for kb, _, bias, off, chosen in tiles:
        s = jnp.dot(kb, qt, preferred_element_type=F32) - bias
        if chosen is None:
            mx = jnp.max(s, axis=0, keepdims=True)
        else:
            mx = jnp.maximum(jnp.where(chosen[0] > 0, jnp.max(s[:half], axis=0, keepdims=True), NEG_INF),
                             jnp.where(chosen[1] > 0, jnp.max(s[half:], axis=0, keepdims=True), NEG_INF))
        scores.append(s)
        tile_max = mx - off if tile_max is None else jnp.maximum(tile_max, mx - off)
    if state is None:
        m_new, l, acc = tile_max, 0.0, 0.0
    else:
        m_old, l_old, acc_old = state
        m_new = jnp.maximum(m_old, tile_max)
        alpha = jnp.exp2(m_old - m_new)
        l, acc = alpha * l_old, alpha * acc_old
    for s, (_, vtb, _, off, chosen) in zip(scores, tiles):
        c0 = c1 = m_new + off
        if chosen is not None:
            c0 = jnp.where(chosen[0] > 0, c0, -NEG_INF)
            c1 = jnp.where(chosen[1] > 0, c1, -NEG_INF)
        p0 = jnp.exp2(s[:half] - c0)
        p1 = jnp.exp2(s[half:] - c1)
        l = l + (jnp.sum(p0, axis=0, keepdims=True) + jnp.sum(p1, axis=0, keepdims=True))
        acc = acc + jnp.dot(vtb, jnp.concatenate([p0, p1], axis=0).astype(BF16), preferred_element_type=F32)
    return m_new, l, acc


BIAS_MID, BIAS_DIAG, BIAS_FIRST = 0, 1, 2


SEL_GROUP = 2


def _sel_win_kernel(flags_ref, q_ref, ks_ref, vs_ref, kw_ref, vw_ref, sel_ref, oc_ref, gl_ref, slope_ref,
                    bias_ref, o_ref, ms_ref, ls_ref, accs_ref, list_ref, *, nq):
    bi, gi, n = pl.program_id(0), pl.program_id(1), pl.program_id(2)
    ns = sel_ref.shape[3]
    qt = q_ref[0, 0, 0]
    slope = slope_ref[0]
    rows_of = lambda p: pl.ds(pl.multiple_of(p * KT, KT), KT)

    nwin = WINDOW // KT
    tiles = []
    for j in range(nwin + 1):
        p = n - nwin + j
        pc = jnp.maximum(p, 0)
        off = slope * ((nwin - j) * KT) + jnp.where(p >= 0, 0.0, -NEG_INF)
        kind = BIAS_FIRST if j == 0 else (BIAS_DIAG if j == nwin else BIAS_MID)
        tiles.append((kw_ref[0, 0, rows_of(pc), :], vw_ref[0, 0, pc], bias_ref[0, kind], off, None))
    _, l_w, acc_w = _flash_step(qt, tiles, None)
    o_w = acc_w * (1.0 / l_w)

    fbase = ((bi * NSA_G + gi) * nq + n) * (ns // 2)

    def scan(p, cnt):
        hit = flags_ref[fbase + p] != 0

        @pl.when(hit)
        def _():
            list_ref[cnt] = p
        return cnt + hit.astype(jnp.int32)

    cnt = lax.fori_loop(0, n, scan, 0)
    list_ref[cnt] = n
    cnt = cnt + 1
    ms_ref[...] = jnp.full(ms_ref.shape, NEG_INF, F32)
    ls_ref[...] = jnp.zeros(ls_ref.shape, F32)
    accs_ref[...] = jnp.zeros(accs_ref.shape, F32)

    def sel_body(it, carry):
        tiles = []
        for u in range(SEL_GROUP):
            e = it * SEL_GROUP + u
            live = e < cnt
            p = list_ref[jnp.minimum(e, cnt - 1)]
            keep = jnp.where(live, 1.0, 0.0)
            chosen = [keep * jnp.concatenate([sel_ref[0, 0, 0, pl.ds(2 * p + i, 1), :]] * NSA_R, axis=1)
                      for i in (0, 1)]
            kind = jnp.where(p == n, BIAS_DIAG, BIAS_MID)
            off = slope * ((n - p) * KT).astype(F32)
            tiles.append((ks_ref[0, 0, rows_of(p), :], vs_ref[0, 0, p], bias_ref[0, kind], off, chosen))
        m, l, acc = _flash_step(qt, tiles, (ms_ref[...], ls_ref[...], accs_ref[...]))
        ms_ref[...] = m
        ls_ref[...] = l
        accs_ref[...] = acc
        return carry

    lax.fori_loop(0, (cnt + SEL_GROUP - 1) // SEL_GROUP, sel_body, 0)
    o_s = accs_ref[...] * (1.0 / ls_ref[...])

    gates = jax.nn.sigmoid(gl_ref[0, 0, 0])
    o = gates[0:1] * oc_ref[0, 0, 0].astype(F32) + gates[1:2] * o_s + gates[2:3] * o_w
    o_ref[0, 0, 0] = o.astype(o_ref.dtype)


def sel_win_attention(pair_flags, q_t, ks, vs_t, kw, vw_t, sel, o_c, gl, slope_row, bias):
    b, g, nq, d, _ = q_t.shape
    t = ks.shape[2]
    ns = sel.shape[3]
    tile = lambda shape: pl.BlockSpec((1, 1, 1) + shape, lambda i, j, n, f: (i, j, n, 0, 0))
    keys = pl.BlockSpec((1, 1, t, d), lambda i, j, n, f: (i, j, 0, 0))
    vals = pl.BlockSpec((1, 1, t // KT, d, KT), lambda i, j, n, f: (i, j, 0, 0, 0))
    grid_spec = pltpu.PrefetchScalarGridSpec(
        num_scalar_prefetch=1,
        grid=(b, g, nq),
        in_specs=[
            tile((d, HQ)), keys, vals, keys, vals,
            tile((ns, QB)), tile((d, HQ)), tile((3, HQ)),
            pl.BlockSpec((1, 1, HQ), lambda i, j, n, f: (j, 0, 0)),
            pl.BlockSpec((1, 3, KT, HQ), lambda i, j, n, f: (j, 0, 0, 0)),
        ],
        out_specs=tile((d, HQ)),
        scratch_shapes=[pltpu.VMEM((1, HQ), F32), pltpu.VMEM((1, HQ), F32), pltpu.VMEM((d, HQ), F32),
                        pltpu.SMEM((nq,), jnp.int32)],
    )
    return pl.pallas_call(
        functools.partial(_sel_win_kernel, nq=nq),
        grid_spec=grid_spec,
        out_shape=jax.ShapeDtypeStruct((b, g, nq, d, HQ), BF16),
        compiler_params=_cparams(("parallel", "parallel", "arbitrary")),
    )(pair_flags, q_t, ks, vs_t, kw, vw_t, sel, o_c, gl, slope_row, bias)


def _row_tile(m):
    return 512 if m % 512 == 0 else m


def even_layer_mixer(h, norm_g, w_in, lb, hgrn_norm, ret_norm, w_out):
    b, t, dm = h.shape
    x2 = h.reshape(b * t, dm)
    tm = _row_tile(b * t)
    proj = norm_matmul(x2, norm_g, w_in.astype(BF16), tm, 1024, BF16)
    o = even_mix(proj.reshape(b, t, EVEN_IN), lb, hgrn_norm, ret_norm, min(t, 512))
    return matmul_residual(o.reshape(b * t, -1), w_out.astype(BF16), x2, tm).reshape(b, t, dm)


def odd_layer_mixer(h, norm_g, w_in, cmp_pos_k, cmp_pos_v, cmp_w1_k, cmp_w2_k, cmp_w1_v, cmp_w2_v, w_out):
    b, t, dm = h.shape
    g_, r_, d = NSA_G, NSA_R, NSA_HD
    nq = t // QB
    ns = t // SLC_BLOCK
    nch = t // CMP_STRIDE
    x2 = h.reshape(b * t, dm)
    tm = _row_tile(b * t)
    q_t, kc, vc, ks, kw, vs_t, vw_t, gl_t = odd_project(h, norm_g, w_in, min(t, 512), (d ** -0.5) * LOG2E)
    chunks = lambda a: a.reshape(b, g_, nch, CMP_STRIDE * d)
    k_cmp, v_cmp = compress(chunks(kc), chunks(vc), cmp_pos_k, cmp_pos_v,
                            cmp_w1_k.astype(BF16), cmp_w1_v.astype(BF16),
                            cmp_w2_k.astype(BF16), cmp_w2_v.astype(BF16))
    v_cmp_t = v_cmp.transpose(0, 1, 3, 2)

    slopes = jnp.exp2(-8.0 * jnp.arange(1, NSA_HEADS + 1, dtype=F32) / NSA_HEADS).reshape(g_, r_)
    slope_row = jnp.repeat(slopes, QB, axis=1).reshape(g_, 1, HQ) * LOG2E
    tq = np.tile(np.arange(QB), r_)[None, :]
    c_rel = np.arange(-(QB // CMP_STRIDE) * (nq - 1), nch)[:, None]
    dist_rel = tq - (c_rel * CMP_STRIDE + CMP_BLOCK - 1)
    cmp_bias = (slope_row * jnp.asarray(dist_rel, F32)[None]
                + jnp.asarray(np.where(dist_rel >= 0, 0.0, -NEG_INF), F32)[None])
    rel = tq - np.arange(KT)[:, None]
    mid = slope_row * jnp.asarray(rel, F32)[None]
    bias = jnp.stack([mid, mid + jnp.asarray(np.where(rel >= 0, 0.0, -NEG_INF), F32),
                      mid + jnp.asarray(np.where(rel < 0, 0.0, -NEG_INF), F32)], axis=1)
    c0 = np.arange(nch) * CMP_STRIDE
    s0 = np.arange(ns) * SLC_BLOCK
    ov = np.maximum(np.minimum(c0[:, None] + CMP_BLOCK, s0[None, :] + SLC_BLOCK)
                    - np.maximum(c0[:, None], s0[None, :]), 0) / CMP_BLOCK
    ov[nch - (CMP_BLOCK // CMP_STRIDE - 1):] = 0.0
    ovt = jnp.asarray(ov.T, BF16)

    o_c, sel, flags = cmp_attention(q_t, k_cmp, v_cmp_t, cmp_bias, ovt)
    pair_flags = flags.reshape(b, g_, nq, ns // 2, 2).max(axis=-1).reshape(-1)
    o = sel_win_attention(pair_flags, q_t, ks, vs_t, kw, vw_t, sel, o_c, gl_t, slope_row, bias)
    o = o.reshape(b, g_, nq, d, r_, QB).transpose(0, 2, 5, 1, 4, 3).reshape(b * t, NSA_HEADS * d)
    return matmul_residual(o, w_out.astype(BF16), x2, tm).reshape(b, t, dm)


def ffn_layer(h, norm_g, w_gate_up, w_down, final_g):
    b, t, dm = h.shape
    m = b * t
    out = ffn(h.reshape(m, dm), norm_g, w_gate_up.astype(BF16), w_down.astype(BF16), final_g,
              1024 if m % 1024 == 0 else m, 256)
    return out.reshape(b, t, dm)


def kernel(x, mix_norm, ffn_norm, final_norm, even_w_in, hgrn_lower_bounds, hgrn_out_norm, ret_out_norm,
           even_w_out, odd_w_in, cmp_pos_k, cmp_pos_v, cmp_w1_k, cmp_w2_k, cmp_w1_v, cmp_w2_v, odd_w_out,
           ffn_w_gate_up, ffn_w_down):
    depth = mix_norm.shape[0]
    lb_all = jnp.cumsum(jax.nn.softmax(hgrn_lower_bounds.astype(F32), axis=0), axis=0)
    h = x
    for layer in range(depth):
        if layer % 2 == 0:
            e = layer // 2
            h = even_layer_mixer(h, mix_norm[layer], even_w_in[e], lb_all[e], hgrn_out_norm[e],
                                 ret_out_norm[e], even_w_out[e])
        else:
            o = layer // 2
            h = odd_layer_mixer(h, mix_norm[layer], odd_w_in[o], cmp_pos_k[o], cmp_pos_v[o], cmp_w1_k[o],
                                cmp_w2_k[o], cmp_w1_v[o], cmp_w2_v[o], odd_w_out[o])
        h = ffn_layer(h, ffn_norm[layer], ffn_w_gate_up[layer], ffn_w_down[layer],
                      final_norm if layer == depth - 1 else None)
    return h
```

```python
import functools

import numpy as np
import jax
import jax.numpy as jnp
from jax import lax
from jax.experimental import pallas as pl
from jax.experimental.pallas import tpu as pltpu

F32 = jnp.float32
BF16 = jnp.bfloat16

D_MODEL = 1024
RMS_EPS = 1e-6
NEG_INF = -1e30
FORCE_SCORE = 1e9

LIN_HEADS = 4
LIN_D = 128
LIN_CHUNK = 64
SUB = 16
EVEN_IN = 8 * LIN_HEADS * LIN_D

NSA_HD = 64
NSA_HEADS = 16
NSA_G = 2
NSA_R = NSA_HEADS // NSA_G
CMP_BLOCK = 32
CMP_STRIDE = 16
SLC_BLOCK = 64
SLC_TOPN = 16
WINDOW = 512
QB = 128
KT = 128
CT = 128
HQ = NSA_R * QB
LOG2E = 1.4426950408889634

FFN_HIDDEN = 2816

VMEM_LIMIT = 56 * 1024 * 1024

_NT = (((1,), (1,)), ((), ()))
_TN = (((0,), (0,)), ((), ()))


def _cparams(sem):
    return pltpu.CompilerParams(dimension_semantics=sem, vmem_limit_bytes=VMEM_LIMIT)


def _rms(x, g):
    ms = jnp.mean(x * x, axis=-1, keepdims=True)
    return x * lax.rsqrt(ms + RMS_EPS) * g


def _silu(x):
    return x * jax.nn.sigmoid(x)


def _norm_matmul_kernel(x_ref, g_ref, w_ref, o_ref, *, tn):
    xn = _rms(x_ref[...], g_ref[...]).astype(BF16)
    for c in range(w_ref.shape[1] // tn):
        cols = slice(c * tn, (c + 1) * tn)
        o_ref[:, cols] = jnp.dot(xn, w_ref[:, cols], preferred_element_type=F32).astype(o_ref.dtype)


def norm_matmul(x, g, w, tm, tn, out_dtype):
    m, k = x.shape
    n = w.shape[1]
    return pl.pallas_call(
        functools.partial(_norm_matmul_kernel, tn=tn),
        grid=(m // tm,),
        in_specs=[
            pl.BlockSpec((tm, k), lambda i: (i, 0)),
            pl.BlockSpec((1, k), lambda i: (0, 0)),
            pl.BlockSpec((k, n), lambda i: (0, 0)),
        ],
        out_specs=pl.BlockSpec((tm, n), lambda i: (i, 0)),
        out_shape=jax.ShapeDtypeStruct((m, n), out_dtype),
        compiler_params=_cparams(("parallel",)),
    )(x, g.reshape(1, k), w)


def _matmul_res_kernel(a_ref, w_ref, r_ref, o_ref):
    o_ref[...] = r_ref[...] + jnp.dot(a_ref[...], w_ref[...], preferred_element_type=F32)


def matmul_residual(a, w, res, tm):
    m, k = a.shape
    n = w.shape[1]
    return pl.pallas_call(
        _matmul_res_kernel,
        grid=(m // tm,),
        in_specs=[
            pl.BlockSpec((tm, k), lambda i: (i, 0)),
            pl.BlockSpec((k, n), lambda i: (0, 0)),
            pl.BlockSpec((tm, n), lambda i: (i, 0)),
        ],
        out_specs=pl.BlockSpec((tm, n), lambda i: (i, 0)),
        out_shape=jax.ShapeDtypeStruct((m, n), F32),
        compiler_params=_cparams(("parallel",)),
    )(a, w, res)


def _ffn_kernel(x_ref, g_ref, wgu_ref, wd_ref, fg_ref, o_ref, xn_ref, acc_ref, *, final_norm):
    j = pl.program_id(1)
    th = wd_ref.shape[0]

    @pl.when(j == 0)
    def _():
        xn_ref[...] = _rms(x_ref[...], g_ref[...]).astype(BF16)
        acc_ref[...] = jnp.zeros_like(acc_ref)

    xn = xn_ref[...]
    gate = jnp.dot(xn, wgu_ref[0, :, :th], preferred_element_type=F32)
    up = jnp.dot(xn, wgu_ref[0, :, th:], preferred_element_type=F32)
    a = (_silu(gate) * up).astype(BF16)
    acc_ref[...] += jnp.dot(a, wd_ref[...], preferred_element_type=F32)

    @pl.when(j == pl.num_programs(1) - 1)
    def _():
        h = x_ref[...] + acc_ref[...]
        if final_norm:
            h = _rms(h, fg_ref[...])
        o_ref[...] = h


def ffn(x, g, w_gate_up, w_down, final_g, tm, th):
    m, k = x.shape
    hid = w_down.shape[0]
    nj = hid // th
    final_norm = final_g is not None
    fg = final_g if final_norm else g
    w_gu = w_gate_up.reshape(k, 2, nj, th).transpose(2, 0, 1, 3).reshape(nj, k, 2 * th)
    return pl.pallas_call(
        functools.partial(_ffn_kernel, final_norm=final_norm),
        grid=(m // tm, nj),
        in_specs=[
            pl.BlockSpec((tm, k), lambda i, j: (i, 0)),
            pl.BlockSpec((1, k), lambda i, j: (0, 0)),
            pl.BlockSpec((1, k, 2 * th), lambda i, j: (j, 0, 0)),
            pl.BlockSpec((th, k), lambda i, j: (j, 0)),
            pl.BlockSpec((1, k), lambda i, j: (0, 0)),
        ],
        out_specs=pl.BlockSpec((tm, k), lambda i, j: (i, 0)),
        out_shape=jax.ShapeDtypeStruct((m, k), F32),
        scratch_shapes=[pltpu.VMEM((tm, k), BF16), pltpu.VMEM((tm, k), F32)],
        compiler_params=_cparams(("parallel", "arbitrary")),
    )(x, g.reshape(1, k), w_gu, w_down, fg.reshape(1, k))


def _cumsum_rows(g):
    c = g.shape[0]
    row = lax.broadcasted_iota(jnp.int32, g.shape, 0)
    s = 1
    while s < c:
        g = g + jnp.where(row >= s, pltpu.roll(g, s, axis=0), 0.0)
        s *= 2
    return g


def _hgrn_chunk(hq, hf, hi, lb, st_ref, h):
    c = LIN_CHUNK
    f = lb + (1.0 - lb) * jax.nn.sigmoid(hf)
    k = 1.0 - f
    q = _silu(hq)
    v = hi
    cum = _cumsum_rows(jnp.log(f))
    st = st_ref[h]
    o_inter = lax.dot_general((q * jnp.exp(cum)).astype(BF16), st.astype(BF16), _NT,
                              preferred_element_type=F32)
    irow = lax.broadcasted_iota(jnp.int32, (SUB, LIN_D), 0)
    krow = lax.broadcasted_iota(jnp.int32, (c, LIN_D), 0)
    lane = lax.broadcasted_iota(jnp.int32, (SUB, c), 1)
    vb = v.astype(BF16)
    attn_rows = []
    for a in range(c // SUB):
        lo = a * SUB
        cum_a = cum[lo:lo + SUB]
        q_a = q[lo:lo + SUB]
        k_a = k[lo:lo + SUB]
        if a > 0:
            ref = cum[lo - 1:lo, :]
            qd = (q_a * jnp.exp(cum_a - ref)).astype(BF16)
            kd = (k * jnp.exp(jnp.where(krow < lo, ref - cum, NEG_INF))).astype(BF16)
            attn = lax.dot_general(qd, kd, _NT, preferred_element_type=F32)
        else:
            attn = jnp.zeros((SUB, c), F32)
        for j in range(SUB):
            dec = jnp.exp(jnp.where(irow >= j, cum_a - cum_a[j:j + 1, :], NEG_INF))
            w = (q_a * k_a[j:j + 1, :]) * dec
            attn = jnp.where(lane == lo + j, jnp.sum(w, axis=-1, keepdims=True), attn)
        attn_rows.append(attn)
    attn = jnp.concatenate(attn_rows, axis=0).astype(BF16)
    o_intra = jnp.dot(attn, vb, preferred_element_type=F32)
    last = cum[c - 1:c, :]
    kd = (k * jnp.exp(last - cum)).astype(BF16)
    st_ref[h] = st * jnp.exp(last) + lax.dot_general(vb, kd, _TN, preferred_element_type=F32)
    return o_intra + o_inter


def _ret_chunk(rq, rk, rv, dec, qdec, kdec, cdec, st_ref, h):
    k = rk * (LIN_D ** -0.5)
    st = st_ref[h]
    qb = rq.astype(BF16)
    vb = rv.astype(BF16)
    attn = lax.dot_general(qb, k.astype(BF16), _NT, preferred_element_type=F32) * dec
    o = jnp.dot(attn.astype(BF16), vb, preferred_element_type=F32)
    o = o + lax.dot_general((rq * qdec).astype(BF16), st.astype(BF16), _NT, preferred_element_type=F32)
    st_ref[h] = st * cdec + lax.dot_general(vb, (k * kdec).astype(BF16), _TN, preferred_element_type=F32)
    return o


def _even_mix_kernel(p_ref, lb_ref, hn_ref, rn_ref, dec_ref, qdec_ref, kdec_ref, cdec_ref,
                     o_ref, st_ref, *, tb):
    @pl.when(pl.program_id(1) == 0)
    def _():
        st_ref[...] = jnp.zeros_like(st_ref)

    w = LIN_HEADS * LIN_D

    def chunk(ci, carry):
        rows = pl.ds(pl.multiple_of(ci * LIN_CHUNK, LIN_CHUNK), LIN_CHUNK)
        for h in range(LIN_HEADS):
            col = lambda part: p_ref[0, rows, part * w + h * LIN_D:part * w + (h + 1) * LIN_D].astype(F32)
            o = _hgrn_chunk(col(0), col(1), col(2), lb_ref[:, h * LIN_D:(h + 1) * LIN_D], st_ref, h)
            o = _rms(o, hn_ref[...]) * _silu(col(3))
            o_ref[0, rows, h * LIN_D:(h + 1) * LIN_D] = o.astype(o_ref.dtype)
        for h in range(LIN_HEADS):
            col = lambda part: p_ref[0, rows, (4 + part) * w + h * LIN_D:(4 + part) * w + (h + 1) * LIN_D].astype(F32)
            o = _ret_chunk(col(0), col(1), col(2), dec_ref[h], qdec_ref[h], kdec_ref[h], cdec_ref[h],
                           st_ref, LIN_HEADS + h)
            o = _rms(o, rn_ref[...]) * _silu(col(3))
            o_ref[0, rows, w + h * LIN_D:w + (h + 1) * LIN_D] = o.astype(o_ref.dtype)
        return carry

    lax.fori_loop(0, tb // LIN_CHUNK, chunk, 0)


def even_mix(proj, lb, hgrn_norm, ret_norm, tb):
    b, t, _ = proj.shape
    c = LIN_CHUNK
    log_gamma = jnp.log(1.0 - jnp.exp2(-5.0 - jnp.arange(LIN_HEADS, dtype=F32)))
    pos = jnp.arange(c, dtype=F32)
    rel = pos[:, None] - pos[None, :]
    dec = jnp.where(rel[None] >= 0, jnp.exp(jnp.maximum(rel, 0.0)[None] * log_gamma[:, None, None]), 0.0)
    ones = jnp.ones((1, 1, LIN_D), F32)
    qdec = jnp.exp((pos + 1.0)[None, :] * log_gamma[:, None])[..., None] * ones
    kdec = jnp.exp((c - 1.0 - pos)[None, :] * log_gamma[:, None])[..., None] * ones
    cdec = jnp.exp(c * log_gamma)[:, None, None] * ones
    const = lambda shape: pl.BlockSpec(shape, lambda i, j: (0,) * len(shape))
    return pl.pallas_call(
        functools.partial(_even_mix_kernel, tb=tb),
        grid=(b, t // tb),
        in_specs=[
            pl.BlockSpec((1, tb, EVEN_IN), lambda i, j: (i, j, 0)),
            const((1, LIN_HEADS * LIN_D)),
            const((1, LIN_D)),
            const((1, LIN_D)),
            const((LIN_HEADS, c, c)),
            const((LIN_HEADS, c, LIN_D)),
            const((LIN_HEADS, c, LIN_D)),
            const((LIN_HEADS, 1, LIN_D)),
        ],
        out_specs=pl.BlockSpec((1, tb, 2 * LIN_HEADS * LIN_D), lambda i, j: (i, j, 0)),
        out_shape=jax.ShapeDtypeStruct((b, t, 2 * LIN_HEADS * LIN_D), BF16),
        scratch_shapes=[pltpu.VMEM((2 * LIN_HEADS, LIN_D, LIN_D), F32)],
        compiler_params=_cparams(("parallel", "arbitrary")),
    )(proj, lb.reshape(1, -1), hgrn_norm.reshape(1, -1), ret_norm.reshape(1, -1), dec, qdec, kdec, cdec)


def _odd_proj_kernel(x_ref, g_ref, wq_ref, wn_ref, wv_ref, wg_ref,
                     q_ref, kc_ref, vc_ref, ks_ref, kw_ref, vs_ref, vw_ref, gl_ref, *, q_scale):
    g_, r_, d = NSA_G, NSA_R, NSA_HD
    xn = _rms(x_ref[0], g_ref[...]).astype(BF16)
    tm = xn.shape[0]
    qt = (lax.dot_general(wq_ref[...], xn, _NT, preferred_element_type=F32) * q_scale).astype(BF16)
    for g in range(g_):
        for r in range(r_):
            rows = slice((g * r_ + r) * d, (g * r_ + r + 1) * d)
            for j in range(tm // QB):
                q_ref[0, g, j, :, r * QB:(r + 1) * QB] = qt[rows, j * QB:(j + 1) * QB]
    nat = jnp.dot(xn, wn_ref[...], preferred_element_type=F32)
    for i, ref in enumerate((kc_ref, vc_ref, ks_ref, kw_ref)):
        for g in range(g_):
            ref[0, g] = nat[:, (i * g_ + g) * d:(i * g_ + g + 1) * d].astype(ref.dtype)
    vt = lax.dot_general(wv_ref[...], xn, _NT, preferred_element_type=F32).astype(BF16)
    for i, ref in enumerate((vs_ref, vw_ref)):
        for g in range(g_):
            for j in range(tm // KT):
                ref[0, g, j] = vt[(i * g_ + g) * d:(i * g_ + g + 1) * d, j * KT:(j + 1) * KT]
    glt = lax.dot_general(wg_ref[...], xn, _NT, preferred_element_type=F32)
    for g in range(g_):
        for br in range(3):
            for r in range(r_):
                row = (g * 3 + br) * r_ + r
                for j in range(tm // QB):
                    gl_ref[0, g, j, br:br + 1, r * QB:(r + 1) * QB] = glt[row:row + 1, j * QB:(j + 1) * QB]


def odd_project(h, norm_g, w_in, tm, q_scale):
    b, t, dm = h.shape
    g_, r_, d = NSA_G, NSA_R, NSA_HD
    kvw = g_ * d
    o0 = NSA_HEADS * d
    part = lambda i: w_in[:, o0 + i * kvw:o0 + (i + 1) * kvw]
    wq = w_in[:, :o0].T.astype(BF16)
    wn = jnp.concatenate([part(0), part(1), part(2), part(4)], axis=1).astype(BF16)
    wv = jnp.concatenate([part(3), part(5)], axis=1).T.astype(BF16)
    wg = w_in[:, o0 + 6 * kvw:o0 + 6 * kvw + 3 * NSA_HEADS]
    wg = wg.reshape(dm, g_, r_, 3).transpose(1, 3, 2, 0).reshape(g_ * 3 * r_, dm).astype(BF16)
    const = lambda a: pl.BlockSpec(a.shape, lambda i, j: (0,) * a.ndim)
    nqt, nkt = tm // QB, tm // KT
    tok = pl.BlockSpec((1, g_, tm, d), lambda i, j: (i, 0, j, 0))
    tok_shape = lambda dt: jax.ShapeDtypeStruct((b, g_, t, d), dt)
    vt_spec = pl.BlockSpec((1, g_, nkt, d, KT), lambda i, j: (i, 0, j, 0, 0))
    vt_shape = jax.ShapeDtypeStruct((b, g_, t // KT, d, KT), BF16)
    return pl.pallas_call(
        functools.partial(_odd_proj_kernel, q_scale=q_scale),
        grid=(b, t // tm),
        in_specs=[pl.BlockSpec((1, tm, dm), lambda i, j: (i, j, 0)), pl.BlockSpec((1, dm), lambda i, j: (0, 0)),
                  const(wq), const(wn), const(wv), const(wg)],
        out_specs=[pl.BlockSpec((1, g_, nqt, d, HQ), lambda i, j: (i, 0, j, 0, 0)),
                   tok, tok, tok, tok, vt_spec, vt_spec,
                   pl.BlockSpec((1, g_, nqt, 3, HQ), lambda i, j: (i, 0, j, 0, 0))],
        out_shape=[jax.ShapeDtypeStruct((b, g_, t // QB, d, HQ), BF16),
                   tok_shape(F32), tok_shape(F32), tok_shape(BF16), tok_shape(BF16), vt_shape, vt_shape,
                   jax.ShapeDtypeStruct((b, g_, t // QB, 3, HQ), F32)],
        compiler_params=_cparams(("parallel", "parallel")),
    )(h, norm_g.reshape(1, dm), wq, wn, wv, wg)


def _compress_kernel(ck_ref, cv_ref, pk_ref, pv_ref, w1k_ref, w1v_ref, w2k_ref, w2v_ref, ok_ref, ov_ref):
    half = CMP_STRIDE * NSA_HD

    def one(c_ref, p_ref, w1_ref, w2_ref):
        ch = c_ref[0, 0]
        a = jnp.dot((ch + p_ref[:, :half]).astype(BF16), w1_ref[:half, :], preferred_element_type=F32)
        b = jnp.dot((ch + p_ref[:, half:]).astype(BF16), w1_ref[half:, :], preferred_element_type=F32)
        nch = a.shape[0]
        pre = a + pltpu.roll(b, nch - 1, axis=0)
        return jnp.dot(_silu(pre).astype(BF16), w2_ref[...], preferred_element_type=F32)

    ok_ref[0, 0] = one(ck_ref, pk_ref, w1k_ref, w2k_ref).astype(ok_ref.dtype)
    ov_ref[0, 0] = one(cv_ref, pv_ref, w1v_ref, w2v_ref).astype(ov_ref.dtype)


def compress(ck, cv, pos_k, pos_v, w1k, w1v, w2k, w2v):
    b, g, nch, width = ck.shape
    blk = pl.BlockSpec((1, 1, nch, width), lambda i, j: (i, j, 0, 0))
    const = lambda shape: pl.BlockSpec(shape, lambda i, j: (0,) * len(shape))
    oblk = pl.BlockSpec((1, 1, nch, NSA_HD), lambda i, j: (i, j, 0, 0))
    oshape = jax.ShapeDtypeStruct((b, g, nch, NSA_HD), BF16)
    return pl.pallas_call(
        _compress_kernel,
        grid=(b, g),
        in_specs=[blk, blk, const((1, 2 * width)), const((1, 2 * width)),
                  const((2 * width, NSA_HD)), const((2 * width, NSA_HD)),
                  const((NSA_HD, NSA_HD)), const((NSA_HD, NSA_HD))],
        out_specs=[oblk, oblk],
        out_shape=[oshape, oshape],
        compiler_params=_cparams(("parallel", "parallel")),
    )(ck, cv, pos_k.reshape(1, -1), pos_v.reshape(1, -1), w1k, w1v, w2k, w2v)


def _cmp_attn_kernel(q_ref, kc_ref, vc_ref, bias_ref, ovt_ref, oc_ref, sel_ref, flag_ref, pslc_ref, *, nq):
    n = pl.program_id(2)
    nc = kc_ref.shape[2]
    ns = ovt_ref.shape[0]
    qt = q_ref[0, 0, 0]
    row0 = pl.multiple_of((QB // CMP_STRIDE) * (nq - 1 - n), QB // CMP_STRIDE)
    tiles_needed = ((QB // CMP_STRIDE) * n + (QB - CMP_BLOCK) // CMP_STRIDE) // CT + 1

    def attend(rows):
        s = jnp.dot(kc_ref[0, 0, :rows, :], qt, preferred_element_type=F32) - bias_ref[0, pl.ds(row0, rows), :]
        m = jnp.maximum(jnp.max(s, axis=0, keepdims=True), 0.1 * NEG_INF)
        e = jnp.exp2(s - m)
        l = jnp.sum(e, axis=0, keepdims=True)
        p = e * (1.0 / jnp.where(l > 0, l, 1.0))
        oc_ref[0, 0, 0] = jnp.dot(vc_ref[0, 0, :, :rows], p.astype(BF16),
                                  preferred_element_type=F32).astype(oc_ref.dtype)
        psum = p[:, 0:QB]
        for r in range(1, NSA_R):
            psum = psum + p[:, r * QB:(r + 1) * QB]
        hi = psum.astype(BF16)
        lo = (psum - hi.astype(F32)).astype(BF16)
        ovt = ovt_ref[:, :rows]
        pslc_ref[...] = (jnp.dot(ovt, hi, preferred_element_type=F32)
                         + jnp.dot(ovt, lo, preferred_element_type=F32))

    for v in range(1, nc // CT + 1):
        pl.when(tiles_needed == v)(functools.partial(attend, v * CT))
    pslc = pslc_ref[...]

    jj = lax.broadcasted_iota(jnp.int32, (ns, QB), 0)
    qblk = (n * QB + lax.broadcasted_iota(jnp.int32, (ns, QB), 1)) >> 6
    forced = (jj == 0) | (jj == qblk) | (jj == qblk - 1)
    score = jnp.where(forced, FORCE_SCORE, jnp.where(jj <= qblk, pslc, NEG_INF))
    sel = jnp.zeros((ns, QB), F32)
    for _ in range(min(SLC_TOPN, ns)):
        mx = jnp.max(score, axis=0, keepdims=True)
        idx = jnp.min(jnp.where(score == mx, jj, ns), axis=0, keepdims=True)
        pick = jj == idx
        sel = jnp.where(pick, 1.0, sel)
        score = jnp.where(pick, -jnp.inf, score)
    sel_ref[0, 0, 0] = sel
    flag_ref[0, 0, 0] = jnp.max(sel, axis=1, keepdims=True).astype(jnp.int32)


def cmp_attention(q_t, k_cmp, v_cmp_t, cmp_bias, ovt):
    b, g, nq, d, _ = q_t.shape
    nc = k_cmp.shape[2]
    ns = ovt.shape[0]
    assert nc % CT == 0 and cmp_bias.shape[1] == (QB // CMP_STRIDE) * (nq - 1) + nc
    tile = lambda shape: pl.BlockSpec((1, 1, 1) + shape, lambda i, j, n: (i, j, n, 0, 0))
    return pl.pallas_call(
        functools.partial(_cmp_attn_kernel, nq=nq),
        grid=(b, g, nq),
        in_specs=[
            tile((d, HQ)),
            pl.BlockSpec((1, 1, nc, d), lambda i, j, n: (i, j, 0, 0)),
            pl.BlockSpec((1, 1, d, nc), lambda i, j, n: (i, j, 0, 0)),
            pl.BlockSpec((1,) + cmp_bias.shape[1:], lambda i, j, n: (j, 0, 0)),
            pl.BlockSpec((ns, nc), lambda i, j, n: (0, 0)),
        ],
        out_specs=[tile((d, HQ)), tile((ns, QB)), tile((ns, 1))],
        out_shape=[
            jax.ShapeDtypeStruct((b, g, nq, d, HQ), BF16),
            jax.ShapeDtypeStruct((b, g, nq, ns, QB), F32),
            jax.ShapeDtypeStruct((b, g, nq, ns, 1), jnp.int32),
        ],
        scratch_shapes=[pltpu.VMEM((ns, QB), F32)],
        compiler_params=_cparams(("parallel", "parallel", "arbitrary")),
    )(q_t, k_cmp, v_cmp_t, cmp_bias, ovt)


def _scores(qt, kb, bias):
    return jnp.dot(kb, qt, preferred_element_type=F32) - bias


def _flash_step(scores, tiles, state):
    half = SLC_BLOCK
    tile_max = None
    for s, (_, off, chosen) in zip(scores, tiles):
        if chosen is None:
            mx = jnp.max(s, axis=0, keepdims=True)
        else:
            mx = jnp.maximum(jnp.where(chosen[0] > 0, jnp.max(s[:half], axis=0, keepdims=True), NEG_INF),
                             jnp.where(chosen[1] > 0, jnp.max(s[half:], axis=0, keepdims=True), NEG_INF))
        tile_max = mx - off if tile_max is None else jnp.maximum(tile_max, mx - off)
    if state is None:
        m_new, l, acc = tile_max, 0.0, 0.0
    else:
        m_old, l_old, acc_old = state
        m_new = jnp.maximum(m_old, tile_max)
        alpha = jnp.exp2(m_old - m_new)
        l, acc = alpha * l_old, alpha * acc_old
    for s, (vtb, off, chosen) in zip(scores, tiles):
        c0 = c1 = m_new + off
        if chosen is not None:
            c0 = jnp.where(chosen[0] > 0, c0, -NEG_INF)
            c1 = jnp.where(chosen[1] > 0, c1, -NEG_INF)
        p0 = jnp.exp2(s[:half] - c0)
        p1 = jnp.exp2(s[half:] - c1)
        l = l + (jnp.sum(p0, axis=0, keepdims=True) + jnp.sum(p1, axis=0, keepdims=True))
        acc = acc + jnp.dot(vtb, jnp.concatenate([p0, p1], axis=0).astype(BF16), preferred_element_type=F32)
    return m_new, l, acc


BIAS_MID, BIAS_DIAG, BIAS_FIRST = 0, 1, 2


SEL_GROUP = 2


def _sel_win_kernel(flags_ref, q_ref, ks_ref, vs_ref, kw_ref, vw_ref, sel_ref, oc_ref, gl_ref, slope_ref,
                    bias_ref, o_ref, ms_ref, ls_ref, accs_ref, sa_ref, sb_ref, list_ref, *, nq):
    bi, gi, n = pl.program_id(0), pl.program_id(1), pl.program_id(2)
    ns = sel_ref.shape[3]
    qt = q_ref[0, 0, 0]
    slope = slope_ref[0]
    rows_of = lambda p: pl.ds(pl.multiple_of(p * KT, KT), KT)

    nwin = WINDOW // KT
    scores, tiles = [], []
    for j in range(nwin + 1):
        p = n - nwin + j
        pc = jnp.maximum(p, 0)
        off = slope * ((nwin - j) * KT) + jnp.where(p >= 0, 0.0, -NEG_INF)
        kind = BIAS_FIRST if j == 0 else (BIAS_DIAG if j == nwin else BIAS_MID)
        scores.append(_scores(qt, kw_ref[0, 0, rows_of(pc), :], bias_ref[0, kind]))
        tiles.append((vw_ref[0, 0, pc], off, None))
    _, l_w, acc_w = _flash_step(scores, tiles, None)
    o_w = acc_w * (1.0 / l_w)

    fbase = ((bi * NSA_G + gi) * nq + n) * (ns // 2)

    def scan(p, cnt):
        hit = flags_ref[fbase + p] != 0

        @pl.when(hit)
        def _():
            list_ref[cnt] = p
        return cnt + hit.astype(jnp.int32)

    cnt = lax.fori_loop(0, n, scan, 0)
    list_ref[cnt] = n
    cnt = cnt + 1
    ms_ref[...] = jnp.full(ms_ref.shape, NEG_INF, F32)
    ls_ref[...] = jnp.zeros(ls_ref.shape, F32)
    accs_ref[...] = jnp.zeros(accs_ref.shape, F32)

    def tile_of(it, u):
        e = it * SEL_GROUP + u
        return list_ref[jnp.minimum(e, cnt - 1)], e < cnt

    def score_stage(it, s_ref):
        for u in range(SEL_GROUP):
            p, _ = tile_of(it, u)
            kind = jnp.where(p == n, BIAS_DIAG, BIAS_MID)
            s_ref[u] = _scores(qt, ks_ref[0, 0, rows_of(p), :], bias_ref[0, kind])

    def softmax_stage(it, s_ref):
        tiles = []
        for u in range(SEL_GROUP):
            p, live = tile_of(it, u)
            keep = jnp.where(live, 1.0, 0.0)
            chosen = [keep * jnp.concatenate([sel_ref[0, 0, 0, pl.ds(2 * p + i, 1), :]] * NSA_R, axis=1)
                      for i in (0, 1)]
            tiles.append((vs_ref[0, 0, p], slope * ((n - p) * KT).astype(F32), chosen))
        m, l, acc = _flash_step([s_ref[u] for u in range(SEL_GROUP)], tiles,
                                (ms_ref[...], ls_ref[...], accs_ref[...]))
        ms_ref[...] = m
        ls_ref[...] = l
        accs_ref[...] = acc

    def sel_body(k, carry):
        score_stage(2 * k + 1, sb_ref)
        softmax_stage(2 * k, sa_ref)
        score_stage(2 * k + 2, sa_ref)
        softmax_stage(2 * k + 1, sb_ref)
        return carry

    score_stage(0, sa_ref)
    lax.fori_loop(0, (cnt + 2 * SEL_GROUP - 1) // (2 * SEL_GROUP), sel_body, 0)
    o_s = accs_ref[...] * (1.0 / ls_ref[...])

    gates = jax.nn.sigmoid(gl_ref[0, 0, 0])
    o = gates[0:1] * oc_ref[0, 0, 0].astype(F32) + gates[1:2] * o_s + gates[2:3] * o_w
    pieces = [jnp.concatenate([o[:, r * QB:(r + 1) * QB], o[:, (r + 1) * QB:(r + 2) * QB]], axis=0).T
              for r in range(0, NSA_R, 2)]
    o_ref[0] = jnp.concatenate(pieces, axis=1).astype(o_ref.dtype)


def sel_win_attention(pair_flags, q_t, ks, vs_t, kw, vw_t, sel, o_c, gl, slope_row, bias):
    b, g, nq, d, _ = q_t.shape
    t = ks.shape[2]
    ns = sel.shape[3]
    tile = lambda shape: pl.BlockSpec((1, 1, 1) + shape, lambda i, j, n, f: (i, j, n, 0, 0))
    keys = pl.BlockSpec((1, 1, t, d), lambda i, j, n, f: (i, j, 0, 0))
    vals = pl.BlockSpec((1, 1, t // KT, d, KT), lambda i, j, n, f: (i, j, 0, 0, 0))
    grid_spec = pltpu.PrefetchScalarGridSpec(
        num_scalar_prefetch=1,
        grid=(b, g, nq),
        in_specs=[
            tile((d, HQ)), keys, vals, keys, vals,
            tile((ns, QB)), tile((d, HQ)), tile((3, HQ)),
            pl.BlockSpec((1, 1, HQ), lambda i, j, n, f: (j, 0, 0)),
            pl.BlockSpec((1, 3, KT, HQ), lambda i, j, n, f: (j, 0, 0, 0)),
        ],
        out_specs=pl.BlockSpec((1, QB, NSA_R * d), lambda i, j, n, f: (i, n, j)),
        scratch_shapes=[pltpu.VMEM((1, HQ), F32), pltpu.VMEM((1, HQ), F32), pltpu.VMEM((d, HQ), F32),
                        pltpu.VMEM((SEL_GROUP, KT, HQ), F32), pltpu.VMEM((SEL_GROUP, KT, HQ), F32),
                        pltpu.SMEM((nq,), jnp.int32)],
    )
    return pl.pallas_call(
        functools.partial(_sel_win_kernel, nq=nq),
        grid_spec=grid_spec,
        out_shape=jax.ShapeDtypeStruct((b, t, g * NSA_R * d), BF16),
        compiler_params=_cparams(("parallel", "parallel", "arbitrary")),
    )(pair_flags, q_t, ks, vs_t, kw, vw_t, sel, o_c, gl, slope_row, bias)


def _row_tile(m):
    return 512 if m % 512 == 0 else m


def even_layer_mixer(h, norm_g, w_in, lb, hgrn_norm, ret_norm, w_out):
    b, t, dm = h.shape
    x2 = h.reshape(b * t, dm)
    tm = _row_tile(b * t)
    proj = norm_matmul(x2, norm_g, w_in.astype(BF16), tm, 1024, BF16)
    o = even_mix(proj.reshape(b, t, EVEN_IN), lb, hgrn_norm, ret_norm, min(t, 512))
    return matmul_residual(o.reshape(b * t, -1), w_out.astype(BF16), x2, tm).reshape(b, t, dm)


def odd_layer_mixer(h, norm_g, w_in, cmp_pos_k, cmp_pos_v, cmp_w1_k, cmp_w2_k, cmp_w1_v, cmp_w2_v, w_out):
    b, t, dm = h.shape
    g_, r_, d = NSA_G, NSA_R, NSA_HD
    nq = t // QB
    ns = t // SLC_BLOCK
    nch = t // CMP_STRIDE
    x2 = h.reshape(b * t, dm)
    tm = _row_tile(b * t)
    q_t, kc, vc, ks, kw, vs_t, vw_t, gl_t = odd_project(h, norm_g, w_in, min(t, 512), (d ** -0.5) * LOG2E)
    chunks = lambda a: a.reshape(b, g_, nch, CMP_STRIDE * d)
    k_cmp, v_cmp = compress(chunks(kc), chunks(vc), cmp_pos_k, cmp_pos_v,
                            cmp_w1_k.astype(BF16), cmp_w1_v.astype(BF16),
                            cmp_w2_k.astype(BF16), cmp_w2_v.astype(BF16))
    v_cmp_t = v_cmp.transpose(0, 1, 3, 2)

    slopes = jnp.exp2(-8.0 * jnp.arange(1, NSA_HEADS + 1, dtype=F32) / NSA_HEADS).reshape(g_, r_)
    slope_row = jnp.repeat(slopes, QB, axis=1).reshape(g_, 1, HQ) * LOG2E
    tq = np.tile(np.arange(QB), r_)[None, :]
    c_rel = np.arange(-(QB // CMP_STRIDE) * (nq - 1), nch)[:, None]
    dist_rel = tq - (c_rel * CMP_STRIDE + CMP_BLOCK - 1)
    cmp_bias = (slope_row * jnp.asarray(dist_rel, F32)[None]
                + jnp.asarray(np.where(dist_rel >= 0, 0.0, -NEG_INF), F32)[None])
    rel = tq - np.arange(KT)[:, None]
    mid = slope_row * jnp.asarray(rel, F32)[None]
    bias = jnp.stack([mid, mid + jnp.asarray(np.where(rel >= 0, 0.0, -NEG_INF), F32),
                      mid + jnp.asarray(np.where(rel < 0, 0.0, -NEG_INF), F32)], axis=1)
    c0 = np.arange(nch) * CMP_STRIDE
    s0 = np.arange(ns) * SLC_BLOCK
    ov = np.maximum(np.minimum(c0[:, None] + CMP_BLOCK, s0[None, :] + SLC_BLOCK)
                    - np.maximum(c0[:, None], s0[None, :]), 0) / CMP_BLOCK
    ov[nch - (CMP_BLOCK // CMP_STRIDE - 1):] = 0.0
    ovt = jnp.asarray(ov.T, BF16)

    o_c, sel, flags = cmp_attention(q_t, k_cmp, v_cmp_t, cmp_bias, ovt)
    pair_flags = flags.reshape(b, g_, nq, ns // 2, 2).max(axis=-1).reshape(-1)
    o = sel_win_attention(pair_flags, q_t, ks, vs_t, kw, vw_t, sel, o_c, gl_t, slope_row, bias)
    return matmul_residual(o.reshape(b * t, NSA_HEADS * d), w_out.astype(BF16), x2, tm).reshape(b, t, dm)


def ffn_layer(h, norm_g, w_gate_up, w_down, final_g):
    b, t, dm = h.shape
    m = b * t
    out = ffn(h.reshape(m, dm), norm_g, w_gate_up.astype(BF16), w_down.astype(BF16), final_g,
              1024 if m % 1024 == 0 else m, 256)
    return out.reshape(b, t, dm)


def kernel(x, mix_norm, ffn_norm, final_norm, even_w_in, hgrn_lower_bounds, hgrn_out_norm, ret_out_norm,
           even_w_out, odd_w_in, cmp_pos_k, cmp_pos_v, cmp_w1_k, cmp_w2_k, cmp_w1_v, cmp_w2_v, odd_w_out,
           ffn_w_gate_up, ffn_w_down):
    depth = mix_norm.shape[0]
    lb_all = jnp.cumsum(jax.nn.softmax(hgrn_lower_bounds.astype(F32), axis=0), axis=0)
    h = x
    for layer in range(depth):
        if layer % 2 == 0:
            e = layer // 2
            h = even_layer_mixer(h, mix_norm[layer], even_w_in[e], lb_all[e], hgrn_out_norm[e],
                                 ret_out_norm[e], even_w_out[e])
        else:
            o = layer // 2
            h = odd_layer_mixer(h, mix_norm[layer], odd_w_in[o], cmp_pos_k[o], cmp_pos_v[o], cmp_w1_k[o],
                                cmp_w2_k[o], cmp_w1_v[o], cmp_w2_v[o], odd_w_out[o])
        h = ffn_layer(h, ffn_norm[layer], ffn_w_gate_up[layer], ffn_w_down[layer],
                      final_norm if layer == depth - 1 else None)
    return h
```

```python
import functools

import numpy as np
import jax
import jax.numpy as jnp
from jax import lax
from jax.experimental import pallas as pl
from jax.experimental.pallas import tpu as pltpu

F32 = jnp.float32
BF16 = jnp.bfloat16

D_MODEL = 1024
RMS_EPS = 1e-6
NEG_INF = -1e30
FORCE_SCORE = 1e9

LIN_HEADS = 4
LIN_D = 128
LIN_CHUNK = 64
SUB = 16
EVEN_IN = 8 * LIN_HEADS * LIN_D

NSA_HD = 64
NSA_HEADS = 16
NSA_G = 2
NSA_R = NSA_HEADS // NSA_G
CMP_BLOCK = 32
CMP_STRIDE = 16
SLC_BLOCK = 64
SLC_TOPN = 16
WINDOW = 512
QB = 128
KT = 128
CT = 128
TILES_PER_STEP = 4
SUBL = 8
HQ = NSA_R * QB
LOG2E = 1.4426950408889634

FFN_HIDDEN = 2816

VMEM_LIMIT = 56 * 1024 * 1024

_NT = (((1,), (1,)), ((), ()))
_TN = (((0,), (0,)), ((), ()))


def _cparams(sem):
    return pltpu.CompilerParams(dimension_semantics=sem, vmem_limit_bytes=VMEM_LIMIT)


def _rms(x, g):
    ms = jnp.mean(x * x, axis=-1, keepdims=True)
    return x * lax.rsqrt(ms + RMS_EPS) * g


def _silu(x):
    return x * jax.nn.sigmoid(x)


def _norm_matmul_kernel(x_ref, g_ref, w_ref, o_ref, *, tn):
    xn = _rms(x_ref[...], g_ref[...]).astype(BF16)
    for c in range(w_ref.shape[1] // tn):
        cols = slice(c * tn, (c + 1) * tn)
        o_ref[:, cols] = jnp.dot(xn, w_ref[:, cols], preferred_element_type=F32).astype(o_ref.dtype)


def norm_matmul(x, g, w, tm, tn, out_dtype):
    m, k = x.shape
    n = w.shape[1]
    return pl.pallas_call(
        functools.partial(_norm_matmul_kernel, tn=tn),
        grid=(m // tm,),
        in_specs=[
            pl.BlockSpec((tm, k), lambda i: (i, 0)),
            pl.BlockSpec((1, k), lambda i: (0, 0)),
            pl.BlockSpec((k, n), lambda i: (0, 0)),
        ],
        out_specs=pl.BlockSpec((tm, n), lambda i: (i, 0)),
        out_shape=jax.ShapeDtypeStruct((m, n), out_dtype),
        compiler_params=_cparams(("parallel",)),
    )(x, g.reshape(1, k), w)


def _ffn_kernel(x_ref, a_ref, wo_ref, g_ref, wgu_ref, wd_ref, fg_ref, o_ref, xn_ref, acc_ref, *, final_norm):
    j = pl.program_id(1)
    th = wd_ref.shape[0]

    @pl.when(j == 0)
    def _():
        h = x_ref[...] + jnp.dot(a_ref[...], wo_ref[...], preferred_element_type=F32)
        xn_ref[...] = _rms(h, g_ref[...]).astype(BF16)
        acc_ref[...] = h

    xn = xn_ref[...]
    gate = jnp.dot(xn, wgu_ref[0, :, :th], preferred_element_type=F32)
    up = jnp.dot(xn, wgu_ref[0, :, th:], preferred_element_type=F32)
    a = (_silu(gate) * up).astype(BF16)
    acc_ref[...] += jnp.dot(a, wd_ref[...], preferred_element_type=F32)

    @pl.when(j == pl.num_programs(1) - 1)
    def _():
        h = acc_ref[...]
        if final_norm:
            h = _rms(h, fg_ref[...])
        o_ref[...] = h


def mixer_out_ffn(x, a, w_out, g, w_gate_up, w_down, final_g, tm, th):
    m, k = x.shape
    hid = w_down.shape[0]
    nj = hid // th
    final_norm = final_g is not None
    fg = final_g if final_norm else g
    w_gu = w_gate_up.reshape(k, 2, nj, th).transpose(2, 0, 1, 3).reshape(nj, k, 2 * th)
    return pl.pallas_call(
        functools.partial(_ffn_kernel, final_norm=final_norm),
        grid=(m // tm, nj),
        in_specs=[
            pl.BlockSpec((tm, k), lambda i, j: (i, 0)),
            pl.BlockSpec((tm, k), lambda i, j: (i, 0)),
            pl.BlockSpec((k, k), lambda i, j: (0, 0)),
            pl.BlockSpec((1, k), lambda i, j: (0, 0)),
            pl.BlockSpec((1, k, 2 * th), lambda i, j: (j, 0, 0)),
            pl.BlockSpec((th, k), lambda i, j: (j, 0)),
            pl.BlockSpec((1, k), lambda i, j: (0, 0)),
        ],
        out_specs=pl.BlockSpec((tm, k), lambda i, j: (i, 0)),
        out_shape=jax.ShapeDtypeStruct((m, k), F32),
        scratch_shapes=[pltpu.VMEM((tm, k), BF16), pltpu.VMEM((tm, k), F32)],
        compiler_params=_cparams(("parallel", "arbitrary")),
    )(x, a, w_out, g.reshape(1, k), w_gu, w_down, fg.reshape(1, k))


def _cumsum_rows(g):
    c = g.shape[0]
    row = lax.broadcasted_iota(jnp.int32, g.shape, 0)
    s = 1
    while s < c:
        g = g + jnp.where(row >= s, pltpu.roll(g, s, axis=0), 0.0)
        s *= 2
    return g


def _hgrn_chunk(hq, hf, hi, lb, st_ref, h):
    c = LIN_CHUNK
    f = lb + (1.0 - lb) * jax.nn.sigmoid(hf)
    k = 1.0 - f
    q = _silu(hq)
    v = hi
    cum = _cumsum_rows(jnp.log(f))
    st = st_ref[h]
    o_inter = lax.dot_general((q * jnp.exp(cum)).astype(BF16), st.astype(BF16), _NT,
                              preferred_element_type=F32)
    irow = lax.broadcasted_iota(jnp.int32, (SUB, LIN_D), 0)
    krow = lax.broadcasted_iota(jnp.int32, (c, LIN_D), 0)
    lane = lax.broadcasted_iota(jnp.int32, (SUB, c), 1)
    vb = v.astype(BF16)
    attn_rows = []
    for a in range(c // SUB):
        lo = a * SUB
        cum_a = cum[lo:lo + SUB]
        q_a = q[lo:lo + SUB]
        k_a = k[lo:lo + SUB]
        if a > 0:
            ref = cum[lo - 1:lo, :]
            qd = (q_a * jnp.exp(cum_a - ref)).astype(BF16)
            kd = (k * jnp.exp(jnp.where(krow < lo, ref - cum, NEG_INF))).astype(BF16)
            attn = lax.dot_general(qd, kd, _NT, preferred_element_type=F32)
        else:
            attn = jnp.zeros((SUB, c), F32)
        for j in range(SUB):
            dec = jnp.exp(jnp.where(irow >= j, cum_a - cum_a[j:j + 1, :], NEG_INF))
            w = (q_a * k_a[j:j + 1, :]) * dec
            attn = jnp.where(lane == lo + j, jnp.sum(w, axis=-1, keepdims=True), attn)
        attn_rows.append(attn)
    attn = jnp.concatenate(attn_rows, axis=0).astype(BF16)
    o_intra = jnp.dot(attn, vb, preferred_element_type=F32)
    last = cum[c - 1:c, :]
    kd = (k * jnp.exp(last - cum)).astype(BF16)
    st_ref[h] = st * jnp.exp(last) + lax.dot_general(vb, kd, _TN, preferred_element_type=F32)
    return o_intra + o_inter


def _ret_chunk(rq, rk, rv, dec, qdec, kdec, cdec, st_ref, h):
    k = rk * (LIN_D ** -0.5)
    st = st_ref[h]
    qb = rq.astype(BF16)
    vb = rv.astype(BF16)
    attn = lax.dot_general(qb, k.astype(BF16), _NT, preferred_element_type=F32) * dec
    o = jnp.dot(attn.astype(BF16), vb, preferred_element_type=F32)
    o = o + lax.dot_general((rq * qdec).astype(BF16), st.astype(BF16), _NT, preferred_element_type=F32)
    st_ref[h] = st * cdec + lax.dot_general(vb, (k * kdec).astype(BF16), _TN, preferred_element_type=F32)
    return o


def _even_mix_kernel(p_ref, lb_ref, hn_ref, rn_ref, dec_ref, qdec_ref, kdec_ref, cdec_ref,
                     o_ref, st_ref, *, tb):
    @pl.when(pl.program_id(1) == 0)
    def _():
        st_ref[...] = jnp.zeros_like(st_ref)

    w = LIN_HEADS * LIN_D

    def chunk(ci, carry):
        rows = pl.ds(pl.multiple_of(ci * LIN_CHUNK, LIN_CHUNK), LIN_CHUNK)
        for h in range(LIN_HEADS):
            col = lambda part: p_ref[0, rows, part * w + h * LIN_D:part * w + (h + 1) * LIN_D].astype(F32)
            o = _hgrn_chunk(col(0), col(1), col(2), lb_ref[:, h * LIN_D:(h + 1) * LIN_D], st_ref, h)
            o = _rms(o, hn_ref[...]) * _silu(col(3))
            o_ref[0, rows, h * LIN_D:(h + 1) * LIN_D] = o.astype(o_ref.dtype)
        for h in range(LIN_HEADS):
            col = lambda part: p_ref[0, rows, (4 + part) * w + h * LIN_D:(4 + part) * w + (h + 1) * LIN_D].astype(F32)
            o = _ret_chunk(col(0), col(1), col(2), dec_ref[h], qdec_ref[h], kdec_ref[h], cdec_ref[h],
                           st_ref, LIN_HEADS + h)
            o = _rms(o, rn_ref[...]) * _silu(col(3))
            o_ref[0, rows, w + h * LIN_D:w + (h + 1) * LIN_D] = o.astype(o_ref.dtype)
        return carry

    lax.fori_loop(0, tb // LIN_CHUNK, chunk, 0)


def even_mix(proj, lb, hgrn_norm, ret_norm, tb):
    b, t, _ = proj.shape
    c = LIN_CHUNK
    log_gamma = jnp.log(1.0 - jnp.exp2(-5.0 - jnp.arange(LIN_HEADS, dtype=F32)))
    pos = jnp.arange(c, dtype=F32)
    rel = pos[:, None] - pos[None, :]
    dec = jnp.where(rel[None] >= 0, jnp.exp(jnp.maximum(rel, 0.0)[None] * log_gamma[:, None, None]), 0.0)
    ones = jnp.ones((1, 1, LIN_D), F32)
    qdec = jnp.exp((pos + 1.0)[None, :] * log_gamma[:, None])[..., None] * ones
    kdec = jnp.exp((c - 1.0 - pos)[None, :] * log_gamma[:, None])[..., None] * ones
    cdec = jnp.exp(c * log_gamma)[:, None, None] * ones
    const = lambda shape: pl.BlockSpec(shape, lambda i, j: (0,) * len(shape))
    return pl.pallas_call(
        functools.partial(_even_mix_kernel, tb=tb),
        grid=(b, t // tb),
        in_specs=[
            pl.BlockSpec((1, tb, EVEN_IN), lambda i, j: (i, j, 0)),
            const((1, LIN_HEADS * LIN_D)),
            const((1, LIN_D)),
            const((1, LIN_D)),
            const((LIN_HEADS, c, c)),
            const((LIN_HEADS, c, LIN_D)),
            const((LIN_HEADS, c, LIN_D)),
            const((LIN_HEADS, 1, LIN_D)),
        ],
        out_specs=pl.BlockSpec((1, tb, 2 * LIN_HEADS * LIN_D), lambda i, j: (i, j, 0)),
        out_shape=jax.ShapeDtypeStruct((b, t, 2 * LIN_HEADS * LIN_D), BF16),
        scratch_shapes=[pltpu.VMEM((2 * LIN_HEADS, LIN_D, LIN_D), F32)],
        compiler_params=_cparams(("parallel", "arbitrary")),
    )(proj, lb.reshape(1, -1), hgrn_norm.reshape(1, -1), ret_norm.reshape(1, -1), dec, qdec, kdec, cdec)


def _odd_proj_kernel(x_ref, g_ref, wq_ref, wn_ref, wv_ref, wg_ref,
                     q_ref, kc_ref, vc_ref, ks_ref, kw_ref, vs_ref, vw_ref, gl_ref, *, q_scale):
    g_, r_, d = NSA_G, NSA_R, NSA_HD
    xn = _rms(x_ref[0], g_ref[...]).astype(BF16)
    tm = xn.shape[0]
    qt = (lax.dot_general(wq_ref[...], xn, _NT, preferred_element_type=F32) * q_scale).astype(BF16)
    for g in range(g_):
        for r in range(r_):
            rows = slice((g * r_ + r) * d, (g * r_ + r + 1) * d)
            for j in range(tm // QB):
                q_ref[0, g, j, :, r * QB:(r + 1) * QB] = qt[rows, j * QB:(j + 1) * QB]
    nat = jnp.dot(xn, wn_ref[...], preferred_element_type=F32)
    for i, ref in enumerate((kc_ref, vc_ref, ks_ref, kw_ref)):
        for g in range(g_):
            ref[0, g] = nat[:, (i * g_ + g) * d:(i * g_ + g + 1) * d].astype(ref.dtype)
    vt = lax.dot_general(wv_ref[...], xn, _NT, preferred_element_type=F32).astype(BF16)
    for i, ref in enumerate((vs_ref, vw_ref)):
        for g in range(g_):
            for j in range(tm // KT):
                ref[0, g, j] = vt[(i * g_ + g) * d:(i * g_ + g + 1) * d, j * KT:(j + 1) * KT]
    glt = lax.dot_general(wg_ref[...], xn, _NT, preferred_element_type=F32)
    for g in range(g_):
        for br in range(3):
            for r in range(r_):
                row = (g * 3 + br) * r_ + r
                for j in range(tm // QB):
                    gl_ref[0, g, j, br:br + 1, r * QB:(r + 1) * QB] = glt[row:row + 1, j * QB:(j + 1) * QB]


def odd_project(h, norm_g, w_in, tm, q_scale):
    b, t, dm = h.shape
    g_, r_, d = NSA_G, NSA_R, NSA_HD
    kvw = g_ * d
    o0 = NSA_HEADS * d
    part = lambda i: w_in[:, o0 + i * kvw:o0 + (i + 1) * kvw]
    wq = w_in[:, :o0].T.astype(BF16)
    wn = jnp.concatenate([part(0), part(1), part(2), part(4)], axis=1).astype(BF16)
    wv = jnp.concatenate([part(3), part(5)], axis=1).T.astype(BF16)
    wg = w_in[:, o0 + 6 * kvw:o0 + 6 * kvw + 3 * NSA_HEADS]
    wg = wg.reshape(dm, g_, r_, 3).transpose(1, 3, 2, 0).reshape(g_ * 3 * r_, dm).astype(BF16)
    const = lambda a: pl.BlockSpec(a.shape, lambda i, j: (0,) * a.ndim)
    nqt, nkt = tm // QB, tm // KT
    tok = pl.BlockSpec((1, g_, tm, d), lambda i, j: (i, 0, j, 0))
    tok_shape = lambda dt: jax.ShapeDtypeStruct((b, g_, t, d), dt)
    vt_spec = pl.BlockSpec((1, g_, nkt, d, KT), lambda i, j: (i, 0, j, 0, 0))
    vt_shape = jax.ShapeDtypeStruct((b, g_, t // KT, d, KT), BF16)
    return pl.pallas_call(
        functools.partial(_odd_proj_kernel, q_scale=q_scale),
        grid=(b, t // tm),
        in_specs=[pl.BlockSpec((1, tm, dm), lambda i, j: (i, j, 0)), pl.BlockSpec((1, dm), lambda i, j: (0, 0)),
                  const(wq), const(wn), const(wv), const(wg)],
        out_specs=[pl.BlockSpec((1, g_, nqt, d, HQ), lambda i, j: (i, 0, j, 0, 0)),
                   tok, tok, tok, tok, vt_spec, vt_spec,
                   pl.BlockSpec((1, g_, nqt, 3, HQ), lambda i, j: (i, 0, j, 0, 0))],
        out_shape=[jax.ShapeDtypeStruct((b, g_, t // QB, d, HQ), BF16),
                   tok_shape(F32), tok_shape(F32), tok_shape(BF16), tok_shape(BF16), vt_shape, vt_shape,
                   jax.ShapeDtypeStruct((b, g_, t // QB, 3, HQ), F32)],
        compiler_params=_cparams(("parallel", "parallel")),
    )(h, norm_g.reshape(1, dm), wq, wn, wv, wg)


def _compress_kernel(ck_ref, cv_ref, pk_ref, pv_ref, w1k_ref, w1v_ref, w2k_ref, w2v_ref, ok_ref, ov_ref):
    half = CMP_STRIDE * NSA_HD

    def one(c_ref, p_ref, w1_ref, w2_ref):
        ch = c_ref[0, 0]
        a = jnp.dot((ch + p_ref[:, :half]).astype(BF16), w1_ref[:half, :], preferred_element_type=F32)
        b = jnp.dot((ch + p_ref[:, half:]).astype(BF16), w1_ref[half:, :], preferred_element_type=F32)
        nch = a.shape[0]
        pre = a + pltpu.roll(b, nch - 1, axis=0)
        return jnp.dot(_silu(pre).astype(BF16), w2_ref[...], preferred_element_type=F32)

    ok_ref[0, 0] = one(ck_ref, pk_ref, w1k_ref, w2k_ref).astype(ok_ref.dtype)
    ov_ref[0, 0] = one(cv_ref, pv_ref, w1v_ref, w2v_ref).astype(ov_ref.dtype)


def compress(ck, cv, pos_k, pos_v, w1k, w1v, w2k, w2v):
    b, g, nch, width = ck.shape
    blk = pl.BlockSpec((1, 1, nch, width), lambda i, j: (i, j, 0, 0))
    const = lambda shape: pl.BlockSpec(shape, lambda i, j: (0,) * len(shape))
    oblk = pl.BlockSpec((1, 1, nch, NSA_HD), lambda i, j: (i, j, 0, 0))
    oshape = jax.ShapeDtypeStruct((b, g, nch, NSA_HD), BF16)
    return pl.pallas_call(
        _compress_kernel,
        grid=(b, g),
        in_specs=[blk, blk, const((1, 2 * width)), const((1, 2 * width)),
                  const((2 * width, NSA_HD)), const((2 * width, NSA_HD)),
                  const((NSA_HD, NSA_HD)), const((NSA_HD, NSA_HD))],
        out_specs=[oblk, oblk],
        out_shape=[oshape, oshape],
        compiler_params=_cparams(("parallel", "parallel")),
    )(ck, cv, pos_k.reshape(1, -1), pos_v.reshape(1, -1), w1k, w1v, w2k, w2v)


def _cmp_attn_kernel(q_ref, kc_ref, vc_ref, bias_ref, ovt_ref, oc_ref, sel_ref, flag_ref, pslc_ref, *, nq):
    nc = kc_ref.shape[2]
    ns = ovt_ref.shape[0]
    tps = q_ref.shape[2]
    lax.fori_loop(0, tps, functools.partial(_cmp_attn_tile, q_ref, kc_ref, vc_ref, bias_ref, ovt_ref, oc_ref,
                                            sel_ref, flag_ref, pslc_ref, nq, nc, ns, tps), 0)


def _cmp_attn_tile(q_ref, kc_ref, vc_ref, bias_ref, ovt_ref, oc_ref, sel_ref, flag_ref, pslc_ref,
                   nq, nc, ns, tps, ti, carry):
    n = pl.program_id(2) * tps + ti
    qt = q_ref[0, 0, ti]
    row0 = pl.multiple_of((QB // CMP_STRIDE) * (nq - 1 - n), QB // CMP_STRIDE)
    tiles_needed = ((QB // CMP_STRIDE) * n + (QB - CMP_BLOCK) // CMP_STRIDE) // CT + 1

    def attend(rows):
        s = jnp.dot(kc_ref[0, 0, :rows, :], qt, preferred_element_type=F32) - bias_ref[0, pl.ds(row0, rows), :]
        m = jnp.maximum(jnp.max(s, axis=0, keepdims=True), 0.1 * NEG_INF)
        e = jnp.exp2(s - m)
        l = jnp.sum(e, axis=0, keepdims=True)
        p = e * (1.0 / jnp.where(l > 0, l, 1.0))
        oc_ref[0, 0, ti] = jnp.dot(vc_ref[0, 0, :, :rows], p.astype(BF16),
                                   preferred_element_type=F32).astype(oc_ref.dtype)
        psum = p[:, 0:QB]
        for r in range(1, NSA_R):
            psum = psum + p[:, r * QB:(r + 1) * QB]
        hi = psum.astype(BF16)
        lo = (psum - hi.astype(F32)).astype(BF16)
        ovt = ovt_ref[:, :rows]
        pslc_ref[...] = (jnp.dot(ovt, hi, preferred_element_type=F32)
                         + jnp.dot(ovt, lo, preferred_element_type=F32))

    for v in range(1, nc // CT + 1):
        pl.when(tiles_needed == v)(functools.partial(attend, v * CT))
    pslc = pslc_ref[...]

    jj = lax.broadcasted_iota(jnp.int32, (ns, QB), 0)
    qblk = (n * QB + lax.broadcasted_iota(jnp.int32, (ns, QB), 1)) >> 6
    forced = (jj == 0) | (jj == qblk) | (jj == qblk - 1)
    score = jnp.where(forced, -jnp.inf, jnp.where(jj <= qblk, pslc, NEG_INF))
    sel = jnp.where(forced, 1.0, 0.0)

    def pick_one(score, sel):
        mx = jnp.max(score, axis=0, keepdims=True)
        idx = jnp.min(jnp.where(score == mx, jj, ns), axis=0, keepdims=True)
        pick = jj == idx
        return jnp.where(pick, -jnp.inf, score), jnp.where(pick, 1.0, sel)

    for _ in range(max(min(SLC_TOPN, ns) - 3, 0)):
        score, sel = pick_one(score, sel)
    score, sel = lax.fori_loop(0, jnp.where(n == 0, 2, 0), lambda _, c: pick_one(*c), (score, sel))
    sel_ref[0, 0, ti] = sel
    flag_ref[0, 0, ti] = jnp.max(sel, axis=1, keepdims=True).astype(jnp.int32)
    return carry


def cmp_attention(q_t, k_cmp, v_cmp_t, cmp_bias, ovt):
    b, g, nq, d, _ = q_t.shape
    nc = k_cmp.shape[2]
    ns = ovt.shape[0]
    assert nc % CT == 0 and cmp_bias.shape[1] == (QB // CMP_STRIDE) * (nq - 1) + nc and nq % TILES_PER_STEP == 0
    tile = lambda shape: pl.BlockSpec((1, 1, TILES_PER_STEP) + shape, lambda i, j, n: (i, j, n, 0, 0))
    return pl.pallas_call(
        functools.partial(_cmp_attn_kernel, nq=nq),
        grid=(b, g, nq // TILES_PER_STEP),
        in_specs=[
            tile((d, HQ)),
            pl.BlockSpec((1, 1, nc, d), lambda i, j, n: (i, j, 0, 0)),
            pl.BlockSpec((1, 1, d, nc), lambda i, j, n: (i, j, 0, 0)),
            pl.BlockSpec((1,) + cmp_bias.shape[1:], lambda i, j, n: (j, 0, 0)),
            pl.BlockSpec((ns, nc), lambda i, j, n: (0, 0)),
        ],
        out_specs=[tile((d, HQ)), tile((ns, QB)), tile((ns, 1))],
        out_shape=[
            jax.ShapeDtypeStruct((b, g, nq, d, HQ), BF16),
            jax.ShapeDtypeStruct((b, g, nq, ns, QB), F32),
            jax.ShapeDtypeStruct((b, g, nq, ns, 1), jnp.int32),
        ],
        scratch_shapes=[pltpu.VMEM((ns, QB), F32)],
        compiler_params=_cparams(("parallel", "parallel", "arbitrary")),
    )(q_t, k_cmp, v_cmp_t, cmp_bias, ovt)


def _scores(qt, kb, bias):
    return jnp.dot(kb, qt, preferred_element_type=F32) - bias


def _flash_step(scores, tiles, state):
    half = SLC_BLOCK
    tile_max = None
    for s, (_, off, chosen) in zip(scores, tiles):
        if chosen is None:
            mx = jnp.max(s, axis=0, keepdims=True)
        else:
            mx = jnp.maximum(jnp.where(chosen[0] > 0, jnp.max(s[:half], axis=0, keepdims=True), NEG_INF),
                             jnp.where(chosen[1] > 0, jnp.max(s[half:], axis=0, keepdims=True), NEG_INF))
        tile_max = mx - off if tile_max is None else jnp.maximum(tile_max, mx - off)
    if state is None:
        m_new, l, acc = tile_max, 0.0, 0.0
    else:
        m_old, l_old, acc_old = state
        m_new = jnp.maximum(m_old, tile_max)
        alpha = jnp.exp2(m_old - m_new)
        l, acc = alpha * l_old, alpha * acc_old
    for s, (vtb, off, chosen) in zip(scores, tiles):
        c0 = c1 = m_new + off
        if chosen is not None:
            c0 = jnp.where(chosen[0] > 0, c0, -NEG_INF)
            c1 = jnp.where(chosen[1] > 0, c1, -NEG_INF)
        p0 = jnp.exp2(s[:half] - c0)
        p1 = jnp.exp2(s[half:] - c1)
        l = l + (p0.reshape(half // SUBL, SUBL, -1).sum(axis=0) + p1.reshape(half // SUBL, SUBL, -1).sum(axis=0))
        acc = acc + jnp.dot(vtb, jnp.concatenate([p0, p1], axis=0).astype(BF16), preferred_element_type=F32)
    return m_new, l, acc


BIAS_MID, BIAS_DIAG, BIAS_FIRST = 0, 1, 2


SEL_GROUP = 2


def _sel_win_kernel(flags_ref, q_ref, ks_ref, vs_ref, kw_ref, vw_ref, sel_ref, oc_ref, gl_ref, slope_ref,
                    bias_ref, o_ref, ms_ref, ls_ref, accs_ref, sa_ref, sb_ref, list_ref, *, nq):
    tps = q_ref.shape[2]
    lax.fori_loop(0, tps, functools.partial(_sel_win_tile, flags_ref, q_ref, ks_ref, vs_ref, kw_ref, vw_ref,
                                            sel_ref, oc_ref, gl_ref, slope_ref, bias_ref, o_ref, ms_ref, ls_ref,
                                            accs_ref, sa_ref, sb_ref, list_ref, nq, tps), 0)


def _sel_win_tile(flags_ref, q_ref, ks_ref, vs_ref, kw_ref, vw_ref, sel_ref, oc_ref, gl_ref, slope_ref,
                  bias_ref, o_ref, ms_ref, ls_ref, accs_ref, sa_ref, sb_ref, list_ref, nq, tps, ti, carry):
    bi, gi, n = pl.program_id(0), pl.program_id(1), pl.program_id(2) * tps + ti
    ns = sel_ref.shape[3]
    qt = q_ref[0, 0, ti]
    slope = slope_ref[0]
    rows_of = lambda p: pl.ds(pl.multiple_of(p * KT, KT), KT)

    nwin = WINDOW // KT
    scores, tiles = [], []
    for j in range(nwin + 1):
        p = n - nwin + j
        pc = jnp.maximum(p, 0)
        off = slope * ((nwin - j) * KT) + jnp.where(p >= 0, 0.0, -NEG_INF)
        kind = BIAS_FIRST if j == 0 else (BIAS_DIAG if j == nwin else BIAS_MID)
        scores.append(_scores(qt, kw_ref[0, 0, rows_of(pc), :], bias_ref[0, kind]))
        tiles.append((vw_ref[0, 0, pc], off, None))
    _, l_w, acc_w = _flash_step(scores, tiles, None)
    o_w = acc_w * (1.0 / jnp.sum(l_w, axis=0, keepdims=True))

    fbase = ((bi * NSA_G + gi) * nq + n) * (ns // 2)

    def scan(p, cnt):
        hit = flags_ref[fbase + p] != 0

        @pl.when(hit)
        def _():
            list_ref[cnt] = p
        return cnt + hit.astype(jnp.int32)

    cnt = lax.fori_loop(0, n, scan, 0)
    list_ref[cnt] = n
    cnt = cnt + 1
    ms_ref[...] = jnp.full(ms_ref.shape, NEG_INF, F32)
    ls_ref[...] = jnp.zeros(ls_ref.shape, F32)
    accs_ref[...] = jnp.zeros(accs_ref.shape, F32)

    def tile_of(it, u):
        e = it * SEL_GROUP + u
        return list_ref[jnp.minimum(e, cnt - 1)], e < cnt

    def score_stage(it, s_ref):
        for u in range(SEL_GROUP):
            p, _ = tile_of(it, u)
            kind = jnp.where(p == n, BIAS_DIAG, BIAS_MID)
            s_ref[u] = _scores(qt, ks_ref[0, 0, rows_of(p), :], bias_ref[0, kind])

    def softmax_stage(it, s_ref):
        tiles = []
        for u in range(SEL_GROUP):
            p, live = tile_of(it, u)
            keep = jnp.where(live, 1.0, 0.0)
            chosen = [keep * jnp.concatenate([sel_ref[0, 0, ti, pl.ds(2 * p + i, 1), :]] * NSA_R, axis=1)
                      for i in (0, 1)]
            tiles.append((vs_ref[0, 0, p], slope * ((n - p) * KT).astype(F32), chosen))
        m, l, acc = _flash_step([s_ref[u] for u in range(SEL_GROUP)], tiles,
                                (ms_ref[...], ls_ref[...], accs_ref[...]))
        ms_ref[...] = m
        ls_ref[...] = l
        accs_ref[...] = acc

    def sel_body(k, carry):
        score_stage(2 * k + 1, sb_ref)
        softmax_stage(2 * k, sa_ref)
        score_stage(2 * k + 2, sa_ref)
        softmax_stage(2 * k + 1, sb_ref)
        return carry

    score_stage(0, sa_ref)
    lax.fori_loop(0, (cnt + 2 * SEL_GROUP - 1) // (2 * SEL_GROUP), sel_body, 0)
    o_s = accs_ref[...] * (1.0 / jnp.sum(ls_ref[...], axis=0, keepdims=True))

    gates = jax.nn.sigmoid(gl_ref[0, 0, ti])
    o = gates[0:1] * oc_ref[0, 0, ti].astype(F32) + gates[1:2] * o_s + gates[2:3] * o_w
    pieces = [jnp.concatenate([o[:, r * QB:(r + 1) * QB], o[:, (r + 1) * QB:(r + 2) * QB]], axis=0).T
              for r in range(0, NSA_R, 2)]
    o_ref[0, pl.ds(pl.multiple_of(ti * QB, QB), QB), :] = jnp.concatenate(pieces, axis=1).astype(o_ref.dtype)
    return carry


def sel_win_attention(pair_flags, q_t, ks, vs_t, kw, vw_t, sel, o_c, gl, slope_row, bias):
    b, g, nq, d, _ = q_t.shape
    t = ks.shape[2]
    ns = sel.shape[3]
    tps = TILES_PER_STEP
    tile = lambda shape: pl.BlockSpec((1, 1, tps) + shape, lambda i, j, n, f: (i, j, n, 0, 0))
    keys = pl.BlockSpec((1, 1, t, d), lambda i, j, n, f: (i, j, 0, 0))
    vals = pl.BlockSpec((1, 1, t // KT, d, KT), lambda i, j, n, f: (i, j, 0, 0, 0))
    grid_spec = pltpu.PrefetchScalarGridSpec(
        num_scalar_prefetch=1,
        grid=(b, g, nq // tps),
        in_specs=[
            tile((d, HQ)), keys, vals, keys, vals,
            tile((ns, QB)), tile((d, HQ)), tile((3, HQ)),
            pl.BlockSpec((1, 1, HQ), lambda i, j, n, f: (j, 0, 0)),
            pl.BlockSpec((1, 3, KT, HQ), lambda i, j, n, f: (j, 0, 0, 0)),
        ],
        out_specs=pl.BlockSpec((1, tps * QB, NSA_R * d), lambda i, j, n, f: (i, n, j)),
        scratch_shapes=[pltpu.VMEM((1, HQ), F32), pltpu.VMEM((SUBL, HQ), F32), pltpu.VMEM((d, HQ), F32),
                        pltpu.VMEM((SEL_GROUP, KT, HQ), F32), pltpu.VMEM((SEL_GROUP, KT, HQ), F32),
                        pltpu.SMEM((nq,), jnp.int32)],
    )
    return pl.pallas_call(
        functools.partial(_sel_win_kernel, nq=nq),
        grid_spec=grid_spec,
        out_shape=jax.ShapeDtypeStruct((b, t, g * NSA_R * d), BF16),
        compiler_params=_cparams(("parallel", "parallel", "arbitrary")),
    )(pair_flags, q_t, ks, vs_t, kw, vw_t, sel, o_c, gl, slope_row, bias)


def _row_tile(m):
    return 512 if m % 512 == 0 else m


def even_layer_heads(h, norm_g, w_in, lb, hgrn_norm, ret_norm):
    b, t, dm = h.shape
    proj = norm_matmul(h.reshape(b * t, dm), norm_g, w_in.astype(BF16), _row_tile(b * t), 1024, BF16)
    o = even_mix(proj.reshape(b, t, EVEN_IN), lb, hgrn_norm, ret_norm, min(t, 512))
    return o.reshape(b * t, -1)


def odd_layer_heads(h, norm_g, w_in, cmp_pos_k, cmp_pos_v, cmp_w1_k, cmp_w2_k, cmp_w1_v, cmp_w2_v):
    b, t, dm = h.shape
    g_, r_, d = NSA_G, NSA_R, NSA_HD
    nq = t // QB
    ns = t // SLC_BLOCK
    nch = t // CMP_STRIDE
    q_t, kc, vc, ks, kw, vs_t, vw_t, gl_t = odd_project(h, norm_g, w_in, min(t, 512), (d ** -0.5) * LOG2E)
    chunks = lambda a: a.reshape(b, g_, nch, CMP_STRIDE * d)
    k_cmp, v_cmp = compress(chunks(kc), chunks(vc), cmp_pos_k, cmp_pos_v,
                            cmp_w1_k.astype(BF16), cmp_w1_v.astype(BF16),
                            cmp_w2_k.astype(BF16), cmp_w2_v.astype(BF16))
    v_cmp_t = v_cmp.transpose(0, 1, 3, 2)

    slopes = jnp.exp2(-8.0 * jnp.arange(1, NSA_HEADS + 1, dtype=F32) / NSA_HEADS).reshape(g_, r_)
    slope_row = jnp.repeat(slopes, QB, axis=1).reshape(g_, 1, HQ) * LOG2E
    tq = np.tile(np.arange(QB), r_)[None, :]
    c_rel = np.arange(-(QB // CMP_STRIDE) * (nq - 1), nch)[:, None]
    dist_rel = tq - (c_rel * CMP_STRIDE + CMP_BLOCK - 1)
    cmp_bias = (slope_row * jnp.asarray(dist_rel, F32)[None]
                + jnp.asarray(np.where(dist_rel >= 0, 0.0, -NEG_INF), F32)[None])
    rel = tq - np.arange(KT)[:, None]
    mid = slope_row * jnp.asarray(rel, F32)[None]
    bias = jnp.stack([mid, mid + jnp.asarray(np.where(rel >= 0, 0.0, -NEG_INF), F32),
                      mid + jnp.asarray(np.where(rel < 0, 0.0, -NEG_INF), F32)], axis=1)
    c0 = np.arange(nch) * CMP_STRIDE
    s0 = np.arange(ns) * SLC_BLOCK
    ov = np.maximum(np.minimum(c0[:, None] + CMP_BLOCK, s0[None, :] + SLC_BLOCK)
                    - np.maximum(c0[:, None], s0[None, :]), 0) / CMP_BLOCK
    ov[nch - (CMP_BLOCK // CMP_STRIDE - 1):] = 0.0
    ovt = jnp.asarray(ov.T, BF16)

    o_c, sel, flags = cmp_attention(q_t, k_cmp, v_cmp_t, cmp_bias, ovt)
    pair_flags = flags.reshape(b, g_, nq, ns // 2, 2).max(axis=-1).reshape(-1)
    o = sel_win_attention(pair_flags, q_t, ks, vs_t, kw, vw_t, sel, o_c, gl_t, slope_row, bias)
    return o.reshape(b * t, NSA_HEADS * d)


def finish_layer(h, heads, w_out, norm_g, w_gate_up, w_down, final_g):
    b, t, dm = h.shape
    m = b * t
    out = mixer_out_ffn(h.reshape(m, dm), heads, w_out.astype(BF16), norm_g, w_gate_up.astype(BF16),
                        w_down.astype(BF16), final_g, 1024 if m % 1024 == 0 else m, 256)
    return out.reshape(b, t, dm)


def kernel(x, mix_norm, ffn_norm, final_norm, even_w_in, hgrn_lower_bounds, hgrn_out_norm, ret_out_norm,
           even_w_out, odd_w_in, cmp_pos_k, cmp_pos_v, cmp_w1_k, cmp_w2_k, cmp_w1_v, cmp_w2_v, odd_w_out,
           ffn_w_gate_up, ffn_w_down):
    depth = mix_norm.shape[0]
    lb_all = jnp.cumsum(jax.nn.softmax(hgrn_lower_bounds.astype(F32), axis=0), axis=0)
    h = x
    for layer in range(depth):
        if layer % 2 == 0:
            e = layer // 2
            heads = even_layer_heads(h, mix_norm[layer], even_w_in[e], lb_all[e], hgrn_out_norm[e], ret_out_norm[e])
            w_out = even_w_out[e]
        else:
            o = layer // 2
            heads = odd_layer_heads(h, mix_norm[layer], odd_w_in[o], cmp_pos_k[o], cmp_pos_v[o], cmp_w1_k[o],
                                    cmp_w2_k[o], cmp_w1_v[o], cmp_w2_v[o])
            w_out = odd_w_out[o]
        h = finish_layer(h, heads, w_out, ffn_norm[layer], ffn_w_gate_up[layer], ffn_w_down[layer],
                         final_norm if layer == depth - 1 else None)
    return h
```

```python
import functools

import numpy as np
import jax
import jax.numpy as jnp
from jax import lax
from jax.experimental import pallas as pl
from jax.experimental.pallas import tpu as pltpu

F32 = jnp.float32
BF16 = jnp.bfloat16

D_MODEL = 1024
RMS_EPS = 1e-6
NEG_INF = -1e30
FORCE_SCORE = 1e9

LIN_HEADS = 4
LIN_D = 128
LIN_CHUNK = 64
SUB = 16
EVEN_IN = 8 * LIN_HEADS * LIN_D

NSA_HD = 64
NSA_HEADS = 16
NSA_G = 2
NSA_R = NSA_HEADS // NSA_G
CMP_BLOCK = 32
CMP_STRIDE = 16
SLC_BLOCK = 64
SLC_TOPN = 16
WINDOW = 512
QB = 128
KT = 128
CT = 128
TILES_PER_STEP = 4
SUBL = 8
SELECTED, WINDOWED = 0, 1
CODE_SHIFT = 10
HQ = NSA_R * QB
LOG2E = 1.4426950408889634

FFN_HIDDEN = 2816

VMEM_LIMIT = 56 * 1024 * 1024

_NT = (((1,), (1,)), ((), ()))
_TN = (((0,), (0,)), ((), ()))


def _cparams(sem):
    return pltpu.CompilerParams(dimension_semantics=sem, vmem_limit_bytes=VMEM_LIMIT)


def _rms(x, g):
    ms = jnp.mean(x * x, axis=-1, keepdims=True)
    return x * lax.rsqrt(ms + RMS_EPS) * g


def _silu(x):
    return x * jax.nn.sigmoid(x)


def _norm_matmul_kernel(x_ref, g_ref, w_ref, o_ref, *, tn):
    xn = _rms(x_ref[...], g_ref[...]).astype(BF16)
    for c in range(w_ref.shape[1] // tn):
        cols = slice(c * tn, (c + 1) * tn)
        o_ref[:, cols] = jnp.dot(xn, w_ref[:, cols], preferred_element_type=F32).astype(o_ref.dtype)


def norm_matmul(x, g, w, tm, tn, out_dtype):
    m, k = x.shape
    n = w.shape[1]
    return pl.pallas_call(
        functools.partial(_norm_matmul_kernel, tn=tn),
        grid=(m // tm,),
        in_specs=[
            pl.BlockSpec((tm, k), lambda i: (i, 0)),
            pl.BlockSpec((1, k), lambda i: (0, 0)),
            pl.BlockSpec((k, n), lambda i: (0, 0)),
        ],
        out_specs=pl.BlockSpec((tm, n), lambda i: (i, 0)),
        out_shape=jax.ShapeDtypeStruct((m, n), out_dtype),
        compiler_params=_cparams(("parallel",)),
    )(x, g.reshape(1, k), w)


def _ffn_kernel(x_ref, a_ref, wo_ref, g_ref, wgu_ref, wd_ref, fg_ref, o_ref, xn_ref, acc_ref, gu_ref, *,
                final_norm, nj):
    j = pl.program_id(1)
    th = wd_ref.shape[0]

    def gate_up(slot):
        gu_ref[slot] = jnp.dot(xn_ref[...], wgu_ref[0], preferred_element_type=F32)

    def down(slot):
        gu = gu_ref[slot]
        a = (_silu(gu[:, :th]) * gu[:, th:]).astype(BF16)
        acc_ref[...] += jnp.dot(a, wd_ref[...], preferred_element_type=F32)

    @pl.when(j == 0)
    def _():
        h = x_ref[...] + jnp.dot(a_ref[...], wo_ref[...], preferred_element_type=F32)
        xn_ref[...] = _rms(h, g_ref[...]).astype(BF16)
        acc_ref[...] = h
        gate_up(0)

    for parity in (0, 1):
        @pl.when((j > 0) & (j < nj) & (j % 2 == parity))
        def _():
            gate_up(parity)
            down(1 - parity)

    @pl.when(j == nj)
    def _():
        down(1 - nj % 2)
        h = acc_ref[...]
        if final_norm:
            h = _rms(h, fg_ref[...])
        o_ref[...] = h


def mixer_out_ffn(x, a, w_out, g, w_gate_up, w_down, final_g, tm, th):
    m, k = x.shape
    hid = w_down.shape[0]
    nj = hid // th
    final_norm = final_g is not None
    fg = final_g if final_norm else g
    w_gu = w_gate_up.reshape(k, 2, nj, th).transpose(2, 0, 1, 3).reshape(nj, k, 2 * th)
    return pl.pallas_call(
        functools.partial(_ffn_kernel, final_norm=final_norm, nj=nj),
        grid=(m // tm, nj + 1),
        in_specs=[
            pl.BlockSpec((tm, k), lambda i, j: (i, 0)),
            pl.BlockSpec((tm, k), lambda i, j: (i, 0)),
            pl.BlockSpec((k, k), lambda i, j: (0, 0)),
            pl.BlockSpec((1, k), lambda i, j: (0, 0)),
            pl.BlockSpec((1, k, 2 * th), lambda i, j: (jnp.minimum(j, nj - 1), 0, 0)),
            pl.BlockSpec((th, k), lambda i, j: (jnp.maximum(j - 1, 0), 0)),
            pl.BlockSpec((1, k), lambda i, j: (0, 0)),
        ],
        out_specs=pl.BlockSpec((tm, k), lambda i, j: (i, 0)),
        out_shape=jax.ShapeDtypeStruct((m, k), F32),
        scratch_shapes=[pltpu.VMEM((tm, k), BF16), pltpu.VMEM((tm, k), F32), pltpu.VMEM((2, tm, 2 * th), F32)],
        compiler_params=_cparams(("parallel", "arbitrary")),
    )(x, a, w_out, g.reshape(1, k), w_gu, w_down, fg.reshape(1, k))


def _cumsum_rows(g):
    c = g.shape[0]
    row = lax.broadcasted_iota(jnp.int32, g.shape, 0)
    s = 1
    while s < c:
        g = g + jnp.where(row >= s, pltpu.roll(g, s, axis=0), 0.0)
        s *= 2
    return g


def _hgrn_chunk(hq, hf, hi, lb, st_ref, h):
    c = LIN_CHUNK
    f = lb + (1.0 - lb) * jax.nn.sigmoid(hf)
    k = 1.0 - f
    q = _silu(hq)
    v = hi
    cum = _cumsum_rows(jnp.log2(f))
    st = st_ref[h]
    o_inter = lax.dot_general((q * jnp.exp2(cum)).astype(BF16), st.astype(BF16), _NT,
                              preferred_element_type=F32)
    irow = lax.broadcasted_iota(jnp.int32, (SUB, LIN_D), 0)
    krow = lax.broadcasted_iota(jnp.int32, (c, LIN_D), 0)
    lane = lax.broadcasted_iota(jnp.int32, (SUB, c), 1)
    vb = v.astype(BF16)
    attn_rows = []
    for a in range(c // SUB):
        lo = a * SUB
        cum_a = cum[lo:lo + SUB]
        q_a = q[lo:lo + SUB]
        k_a = k[lo:lo + SUB]
        if a > 0:
            ref = cum[lo - 1:lo, :]
            qd = (q_a * jnp.exp2(cum_a - ref)).astype(BF16)
            kd = (k * jnp.exp2(jnp.where(krow < lo, ref - cum, NEG_INF))).astype(BF16)
            attn = lax.dot_general(qd, kd, _NT, preferred_element_type=F32)
        else:
            attn = jnp.zeros((SUB, c), F32)
        for j in range(SUB):
            dec = jnp.exp2(jnp.where(irow >= j, cum_a - cum_a[j:j + 1, :], NEG_INF))
            w = (q_a * k_a[j:j + 1, :]) * dec
            attn = jnp.where(lane == lo + j, jnp.sum(w, axis=-1, keepdims=True), attn)
        attn_rows.append(attn)
    attn = jnp.concatenate(attn_rows, axis=0).astype(BF16)
    o_intra = jnp.dot(attn, vb, preferred_element_type=F32)
    last = cum[c - 1:c, :]
    kd = (k * jnp.exp2(last - cum)).astype(BF16)
    st_ref[h] = st * jnp.exp2(last) + lax.dot_general(vb, kd, _TN, preferred_element_type=F32)
    return o_intra + o_inter


def _ret_chunk(rq, rk, rv, dec, qdec, kdec, cdec, st_ref, h):
    k = rk * (LIN_D ** -0.5)
    st = st_ref[h]
    qb = rq.astype(BF16)
    vb = rv.astype(BF16)
    attn = lax.dot_general(qb, k.astype(BF16), _NT, preferred_element_type=F32) * dec
    o = jnp.dot(attn.astype(BF16), vb, preferred_element_type=F32)
    o = o + lax.dot_general((rq * qdec).astype(BF16), st.astype(BF16), _NT, preferred_element_type=F32)
    st_ref[h] = st * cdec + lax.dot_general(vb, (k * kdec).astype(BF16), _TN, preferred_element_type=F32)
    return o


def _even_mix_kernel(p_ref, lb_ref, hn_ref, rn_ref, dec_ref, qdec_ref, kdec_ref, cdec_ref,
                     o_ref, st_ref, *, tb):
    @pl.when(pl.program_id(1) == 0)
    def _():
        st_ref[...] = jnp.zeros_like(st_ref)

    w = LIN_HEADS * LIN_D

    def chunk(ci, carry):
        rows = pl.ds(pl.multiple_of(ci * LIN_CHUNK, LIN_CHUNK), LIN_CHUNK)
        for h in range(LIN_HEADS):
            col = lambda part: p_ref[0, rows, part * w + h * LIN_D:part * w + (h + 1) * LIN_D].astype(F32)
            o = _hgrn_chunk(col(0), col(1), col(2), lb_ref[:, h * LIN_D:(h + 1) * LIN_D], st_ref, h)
            o = _rms(o, hn_ref[...]) * _silu(col(3))
            o_ref[0, rows, h * LIN_D:(h + 1) * LIN_D] = o.astype(o_ref.dtype)
        for h in range(LIN_HEADS):
            col = lambda part: p_ref[0, rows, (4 + part) * w + h * LIN_D:(4 + part) * w + (h + 1) * LIN_D].astype(F32)
            o = _ret_chunk(col(0), col(1), col(2), dec_ref[h], qdec_ref[h], kdec_ref[h], cdec_ref[h],
                           st_ref, LIN_HEADS + h)
            o = _rms(o, rn_ref[...]) * _silu(col(3))
            o_ref[0, rows, w + h * LIN_D:w + (h + 1) * LIN_D] = o.astype(o_ref.dtype)
        return carry

    lax.fori_loop(0, tb // LIN_CHUNK, chunk, 0)


def even_mix(proj, lb, hgrn_norm, ret_norm, tb):
    b, t, _ = proj.shape
    c = LIN_CHUNK
    log_gamma = jnp.log(1.0 - jnp.exp2(-5.0 - jnp.arange(LIN_HEADS, dtype=F32)))
    pos = jnp.arange(c, dtype=F32)
    rel = pos[:, None] - pos[None, :]
    dec = jnp.where(rel[None] >= 0, jnp.exp(jnp.maximum(rel, 0.0)[None] * log_gamma[:, None, None]), 0.0)
    ones = jnp.ones((1, 1, LIN_D), F32)
    qdec = jnp.exp((pos + 1.0)[None, :] * log_gamma[:, None])[..., None] * ones
    kdec = jnp.exp((c - 1.0 - pos)[None, :] * log_gamma[:, None])[..., None] * ones
    cdec = jnp.exp(c * log_gamma)[:, None, None] * ones
    const = lambda shape: pl.BlockSpec(shape, lambda i, j: (0,) * len(shape))
    return pl.pallas_call(
        functools.partial(_even_mix_kernel, tb=tb),
        grid=(b, t // tb),
        in_specs=[
            pl.BlockSpec((1, tb, EVEN_IN), lambda i, j: (i, j, 0)),
            const((1, LIN_HEADS * LIN_D)),
            const((1, LIN_D)),
            const((1, LIN_D)),
            const((LIN_HEADS, c, c)),
            const((LIN_HEADS, c, LIN_D)),
            const((LIN_HEADS, c, LIN_D)),
            const((LIN_HEADS, 1, LIN_D)),
        ],
        out_specs=pl.BlockSpec((1, tb, 2 * LIN_HEADS * LIN_D), lambda i, j: (i, j, 0)),
        out_shape=jax.ShapeDtypeStruct((b, t, 2 * LIN_HEADS * LIN_D), BF16),
        scratch_shapes=[pltpu.VMEM((2 * LIN_HEADS, LIN_D, LIN_D), F32)],
        compiler_params=_cparams(("parallel", "arbitrary")),
    )(proj, lb.reshape(1, -1), hgrn_norm.reshape(1, -1), ret_norm.reshape(1, -1), dec, qdec, kdec, cdec)


def _odd_proj_kernel(x_ref, g_ref, wq_ref, wn_ref, wv_ref, wg_ref,
                     q_ref, kc_ref, vc_ref, kk_ref, vv_ref, gl_ref, *, q_scale):
    g_, r_, d = NSA_G, NSA_R, NSA_HD
    xn = _rms(x_ref[0], g_ref[...]).astype(BF16)
    tm = xn.shape[0]
    qt = (lax.dot_general(wq_ref[...], xn, _NT, preferred_element_type=F32) * q_scale).astype(BF16)
    for g in range(g_):
        for r in range(r_):
            rows = slice((g * r_ + r) * d, (g * r_ + r + 1) * d)
            for j in range(tm // QB):
                q_ref[0, g, j, :, r * QB:(r + 1) * QB] = qt[rows, j * QB:(j + 1) * QB]
    nat = jnp.dot(xn, wn_ref[...], preferred_element_type=F32)
    col = lambda i, g: nat[:, (i * g_ + g) * d:(i * g_ + g + 1) * d]
    for g in range(g_):
        kc_ref[0, g] = col(0, g)
        vc_ref[0, g] = col(1, g)
        kk_ref[0, g, SELECTED] = col(2, g).astype(BF16)
        kk_ref[0, g, WINDOWED] = col(3, g).astype(BF16)
    vt = lax.dot_general(wv_ref[...], xn, _NT, preferred_element_type=F32).astype(BF16)
    for i in (SELECTED, WINDOWED):
        for g in range(g_):
            for j in range(tm // KT):
                vv_ref[0, g, i, j] = vt[(i * g_ + g) * d:(i * g_ + g + 1) * d, j * KT:(j + 1) * KT]
    glt = lax.dot_general(wg_ref[...], xn, _NT, preferred_element_type=F32)
    for g in range(g_):
        for br in range(3):
            for r in range(r_):
                row = (g * 3 + br) * r_ + r
                for j in range(tm // QB):
                    gl_ref[0, g, j, br:br + 1, r * QB:(r + 1) * QB] = glt[row:row + 1, j * QB:(j + 1) * QB]


def odd_project(h, norm_g, w_in, tm, q_scale):
    b, t, dm = h.shape
    g_, r_, d = NSA_G, NSA_R, NSA_HD
    kvw = g_ * d
    o0 = NSA_HEADS * d
    part = lambda i: w_in[:, o0 + i * kvw:o0 + (i + 1) * kvw]
    wq = w_in[:, :o0].T.astype(BF16)
    wn = jnp.concatenate([part(0), part(1), part(2), part(4)], axis=1).astype(BF16)
    wv = jnp.concatenate([part(3), part(5)], axis=1).T.astype(BF16)
    wg = w_in[:, o0 + 6 * kvw:o0 + 6 * kvw + 3 * NSA_HEADS]
    wg = wg.reshape(dm, g_, r_, 3).transpose(1, 3, 2, 0).reshape(g_ * 3 * r_, dm).astype(BF16)
    const = lambda a: pl.BlockSpec(a.shape, lambda i, j: (0,) * a.ndim)
    nqt, nkt = tm // QB, tm // KT
    tok = pl.BlockSpec((1, g_, tm, d), lambda i, j: (i, 0, j, 0))
    tok_shape = jax.ShapeDtypeStruct((b, g_, t, d), F32)
    kk_spec = pl.BlockSpec((1, g_, 2, tm, d), lambda i, j: (i, 0, 0, j, 0))
    kk_shape = jax.ShapeDtypeStruct((b, g_, 2, t, d), BF16)
    vv_spec = pl.BlockSpec((1, g_, 2, nkt, d, KT), lambda i, j: (i, 0, 0, j, 0, 0))
    vv_shape = jax.ShapeDtypeStruct((b, g_, 2, t // KT, d, KT), BF16)
    return pl.pallas_call(
        functools.partial(_odd_proj_kernel, q_scale=q_scale),
        grid=(b, t // tm),
        in_specs=[pl.BlockSpec((1, tm, dm), lambda i, j: (i, j, 0)), pl.BlockSpec((1, dm), lambda i, j: (0, 0)),
                  const(wq), const(wn), const(wv), const(wg)],
        out_specs=[pl.BlockSpec((1, g_, nqt, d, HQ), lambda i, j: (i, 0, j, 0, 0)),
                   tok, tok, kk_spec, vv_spec,
                   pl.BlockSpec((1, g_, nqt, 3, HQ), lambda i, j: (i, 0, j, 0, 0))],
        out_shape=[jax.ShapeDtypeStruct((b, g_, t // QB, d, HQ), BF16),
                   tok_shape, tok_shape, kk_shape, vv_shape,
                   jax.ShapeDtypeStruct((b, g_, t // QB, 3, HQ), F32)],
        compiler_params=_cparams(("parallel", "parallel")),
    )(h, norm_g.reshape(1, dm), wq, wn, wv, wg)


def _compress_kernel(ck_ref, cv_ref, pk_ref, pv_ref, w1k_ref, w1v_ref, w2k_ref, w2v_ref, ok_ref, ov_ref):
    half = CMP_STRIDE * NSA_HD

    def one(c_ref, p_ref, w1_ref, w2_ref):
        ch = c_ref[0, 0]
        a = jnp.dot((ch + p_ref[:, :half]).astype(BF16), w1_ref[:half, :], preferred_element_type=F32)
        b = jnp.dot((ch + p_ref[:, half:]).astype(BF16), w1_ref[half:, :], preferred_element_type=F32)
        nch = a.shape[0]
        pre = a + pltpu.roll(b, nch - 1, axis=0)
        return jnp.dot(_silu(pre).astype(BF16), w2_ref[...], preferred_element_type=F32)

    ok_ref[0, 0] = one(ck_ref, pk_ref, w1k_ref, w2k_ref).astype(ok_ref.dtype)
    ov_ref[0, 0] = one(cv_ref, pv_ref, w1v_ref, w2v_ref).astype(ov_ref.dtype)


def compress(ck, cv, pos_k, pos_v, w1k, w1v, w2k, w2v):
    b, g, nch, width = ck.shape
    blk = pl.BlockSpec((1, 1, nch, width), lambda i, j: (i, j, 0, 0))
    const = lambda shape: pl.BlockSpec(shape, lambda i, j: (0,) * len(shape))
    oblk = pl.BlockSpec((1, 1, nch, NSA_HD), lambda i, j: (i, j, 0, 0))
    oshape = jax.ShapeDtypeStruct((b, g, nch, NSA_HD), BF16)
    return pl.pallas_call(
        _compress_kernel,
        grid=(b, g),
        in_specs=[blk, blk, const((1, 2 * width)), const((1, 2 * width)),
                  const((2 * width, NSA_HD)), const((2 * width, NSA_HD)),
                  const((NSA_HD, NSA_HD)), const((NSA_HD, NSA_HD))],
        out_specs=[oblk, oblk],
        out_shape=[oshape, oshape],
        compiler_params=_cparams(("parallel", "parallel")),
    )(ck, cv, pos_k.reshape(1, -1), pos_v.reshape(1, -1), w1k, w1v, w2k, w2v)


def _cmp_attn_kernel(q_ref, kc_ref, vc_ref, bias_ref, ovt_ref, oc_ref, sel_ref, flag_ref, pslc_ref, *, nq):
    nc = kc_ref.shape[2]
    ns = ovt_ref.shape[0]
    tps = q_ref.shape[2]
    lax.fori_loop(0, tps // 2, functools.partial(_cmp_attn_pair, q_ref, kc_ref, vc_ref, bias_ref, ovt_ref, oc_ref,
                                                 sel_ref, flag_ref, pslc_ref, nq, nc, ns, tps), 0)


def _cmp_attn_pair(q_ref, kc_ref, vc_ref, bias_ref, ovt_ref, oc_ref, sel_ref, flag_ref, pslc_ref,
                   nq, nc, ns, tps, pi, carry):
    n0 = pl.program_id(2) * tps + 2 * pi
    for u in (0, 1):
        _cmp_attend(q_ref, kc_ref, vc_ref, bias_ref, ovt_ref, oc_ref, pslc_ref, nq, nc, 2 * pi + u, n0 + u, u)
    pslc = pslc_ref[...]

    jj = lax.broadcasted_iota(jnp.int32, (2, ns, QB), 1)
    tile = lax.broadcasted_iota(jnp.int32, (2, ns, QB), 0)
    qblk = ((n0 + tile) * QB + lax.broadcasted_iota(jnp.int32, (2, ns, QB), 2)) >> 6
    forced = (jj == 0) | (jj == qblk) | (jj == qblk - 1)
    score = jnp.where(forced, -jnp.inf, jnp.where(jj <= qblk, pslc, NEG_INF))
    sel = jnp.where(forced, 1.0, 0.0)

    def pick_one(score, sel):
        mx = jnp.max(score, axis=1, keepdims=True)
        idx = jnp.min(jnp.where(score == mx, jj, ns), axis=1, keepdims=True)
        pick = jj == idx
        return jnp.where(pick, -jnp.inf, score), jnp.where(pick, 1.0, sel)

    for _ in range(max(min(SLC_TOPN, ns) - 3, 0)):
        score, sel = pick_one(score, sel)
    score, sel = lax.fori_loop(0, jnp.where(n0 == 0, 2, 0), lambda _, c: pick_one(*c), (score, sel))
    for u in (0, 1):
        sel_ref[0, 0, 2 * pi + u] = sel[u]
        flag_ref[0, 0, 2 * pi + u] = jnp.max(sel[u], axis=1, keepdims=True).astype(jnp.int32)
    return carry


def _cmp_attend(q_ref, kc_ref, vc_ref, bias_ref, ovt_ref, oc_ref, pslc_ref, nq, nc, ti, n, slot):
    qt = q_ref[0, 0, ti]
    row0 = pl.multiple_of((QB // CMP_STRIDE) * (nq - 1 - n), QB // CMP_STRIDE)
    tiles_needed = ((QB // CMP_STRIDE) * n + (QB - CMP_BLOCK) // CMP_STRIDE) // CT + 1

    def attend(rows):
        s = jnp.dot(kc_ref[0, 0, :rows, :], qt, preferred_element_type=F32) - bias_ref[0, pl.ds(row0, rows), :]
        m = jnp.maximum(jnp.max(s, axis=0, keepdims=True), 0.1 * NEG_INF)
        e = jnp.exp2(s - m)
        l = jnp.sum(e, axis=0, keepdims=True)
        p = e * (1.0 / jnp.where(l > 0, l, 1.0))
        oc_ref[0, 0, ti] = jnp.dot(vc_ref[0, 0, :, :rows], p.astype(BF16),
                                   preferred_element_type=F32).astype(oc_ref.dtype)
        psum = p[:, 0:QB]
        for r in range(1, NSA_R):
            psum = psum + p[:, r * QB:(r + 1) * QB]
        hi = psum.astype(BF16)
        lo = (psum - hi.astype(F32)).astype(BF16)
        ovt = ovt_ref[:, :rows]
        pslc_ref[slot] = (jnp.dot(ovt, hi, preferred_element_type=F32)
                          + jnp.dot(ovt, lo, preferred_element_type=F32))

    for v in range(1, nc // CT + 1):
        pl.when(tiles_needed == v)(functools.partial(attend, v * CT))


def cmp_attention(q_t, k_cmp, v_cmp_t, cmp_bias, ovt):
    b, g, nq, d, _ = q_t.shape
    nc = k_cmp.shape[2]
    ns = ovt.shape[0]
    assert nc % CT == 0 and cmp_bias.shape[1] == (QB // CMP_STRIDE) * (nq - 1) + nc
    assert nq % TILES_PER_STEP == 0 and TILES_PER_STEP % 2 == 0
    tile = lambda shape: pl.BlockSpec((1, 1, TILES_PER_STEP) + shape, lambda i, j, n: (i, j, n, 0, 0))
    return pl.pallas_call(
        functools.partial(_cmp_attn_kernel, nq=nq),
        grid=(b, g, nq // TILES_PER_STEP),
        in_specs=[
            tile((d, HQ)),
            pl.BlockSpec((1, 1, nc, d), lambda i, j, n: (i, j, 0, 0)),
            pl.BlockSpec((1, 1, d, nc), lambda i, j, n: (i, j, 0, 0)),
            pl.BlockSpec((1,) + cmp_bias.shape[1:], lambda i, j, n: (j, 0, 0)),
            pl.BlockSpec((ns, nc), lambda i, j, n: (0, 0)),
        ],
        out_specs=[tile((d, HQ)), tile((ns, QB)), tile((ns, 1))],
        out_shape=[
            jax.ShapeDtypeStruct((b, g, nq, d, HQ), BF16),
            jax.ShapeDtypeStruct((b, g, nq, ns, QB), F32),
            jax.ShapeDtypeStruct((b, g, nq, ns, 1), jnp.int32),
        ],
        scratch_shapes=[pltpu.VMEM((2, ns, QB), F32)],
        compiler_params=_cparams(("parallel", "parallel", "arbitrary")),
    )(q_t, k_cmp, v_cmp_t, cmp_bias, ovt)


def _scores(qt, kb, bias):
    return jnp.dot(kb, qt, preferred_element_type=F32) - bias


def _flash_step(scores, tiles, state):
    half = SLC_BLOCK
    tile_max = None
    for s, (_, off, chosen) in zip(scores, tiles):
        if chosen is None:
            mx = jnp.max(s, axis=0, keepdims=True)
        else:
            mx = jnp.maximum(jnp.where(chosen[0] > 0, jnp.max(s[:half], axis=0, keepdims=True), NEG_INF),
                             jnp.where(chosen[1] > 0, jnp.max(s[half:], axis=0, keepdims=True), NEG_INF))
        tile_max = mx - off if tile_max is None else jnp.maximum(tile_max, mx - off)
    if state is None:
        m_new, l, acc = tile_max, 0.0, 0.0
    else:
        m_old, l_old, acc_old = state
        m_new = jnp.maximum(m_old, tile_max)
        alpha = jnp.exp2(m_old - m_new)
        l, acc = alpha * l_old, alpha * acc_old
    for s, (vtb, off, chosen) in zip(scores, tiles):
        c0 = c1 = m_new + off
        if chosen is not None:
            c0 = jnp.where(chosen[0] > 0, c0, -NEG_INF)
            c1 = jnp.where(chosen[1] > 0, c1, -NEG_INF)
        p0 = jnp.exp2(s[:half] - c0)
        p1 = jnp.exp2(s[half:] - c1)
        l = l + (p0.reshape(half // SUBL, SUBL, -1).sum(axis=0) + p1.reshape(half // SUBL, SUBL, -1).sum(axis=0))
        acc = acc + jnp.dot(vtb, jnp.concatenate([p0, p1], axis=0).astype(BF16), preferred_element_type=F32)
    return m_new, l, acc


BIAS_MID, BIAS_DIAG, BIAS_FIRST = 0, 1, 2


SEL_GROUP = 2


def _sel_win_kernel(lists_ref, nstage_ref, q_ref, kk_ref, vv_ref, sel_ref, oc_ref, gl_ref, slope_ref,
                    bias_ref, o_ref, m_ref, l_ref, acc_ref, sa_ref, sb_ref, *, nq, stride):
    tps = q_ref.shape[2]
    lax.fori_loop(0, tps, functools.partial(_sel_win_tile, lists_ref, nstage_ref, q_ref, kk_ref, vv_ref, sel_ref,
                                            oc_ref, gl_ref, slope_ref, bias_ref, o_ref, m_ref, l_ref, acc_ref,
                                            sa_ref, sb_ref, nq, stride, tps), 0)


def _sel_win_tile(lists_ref, nstage_ref, q_ref, kk_ref, vv_ref, sel_ref, oc_ref, gl_ref, slope_ref,
                  bias_ref, o_ref, m_ref, l_ref, acc_ref, sa_ref, sb_ref, nq, stride, tps, ti, carry):
    bi, gi, n = pl.program_id(0), pl.program_id(1), pl.program_id(2) * tps + ti
    qt = q_ref[0, 0, ti]
    slope = slope_ref[0]
    tile_id = (bi * NSA_G + gi) * nq + n
    base = tile_id * stride
    nstage = nstage_ref[tile_id]
    m_ref[...] = jnp.full(m_ref.shape, NEG_INF, F32)
    l_ref[...] = jnp.zeros(l_ref.shape, F32)
    acc_ref[...] = jnp.zeros(acc_ref.shape, F32)

    def entry(it, u):
        code = lists_ref[base + jnp.minimum(it, nstage) * SEL_GROUP + u]
        c = jnp.maximum(code, 0)
        return c >> CODE_SHIFT, c & ((1 << CODE_SHIFT) - 1), code >= 0

    def score_stage(it, s_ref):
        for u in range(SEL_GROUP):
            br, p, _ = entry(it, u)
            first = (br == WINDOWED) & (p == n - WINDOW // KT)
            kind = jnp.where(p == n, BIAS_DIAG, jnp.where(first, BIAS_FIRST, BIAS_MID))
            rows = pl.ds(pl.multiple_of(p * KT, KT), KT)
            s_ref[u] = _scores(qt, kk_ref[0, 0, br, rows, :], bias_ref[0, kind])

    def softmax_stage(it, s_ref):
        tiles = []
        for u in range(SEL_GROUP):
            br, p, live = entry(it, u)
            keep = jnp.where(live, 1.0, 0.0)
            everyone = jnp.where(br == WINDOWED, 1.0, 0.0)
            chosen = [keep * jnp.maximum(jnp.concatenate([sel_ref[0, 0, ti, pl.ds(2 * p + i, 1), :]] * NSA_R,
                                                         axis=1), everyone) for i in (0, 1)]
            tiles.append((vv_ref[0, 0, br, p], slope * ((n - p) * KT).astype(F32), chosen))
        sb, _, _ = entry(it, 0)
        m, l, acc = _flash_step([s_ref[u] for u in range(SEL_GROUP)], tiles, (m_ref[sb], l_ref[sb], acc_ref[sb]))
        m_ref[sb] = m
        l_ref[sb] = l
        acc_ref[sb] = acc

    def body(k, carry):
        score_stage(2 * k + 1, sb_ref)
        softmax_stage(2 * k, sa_ref)
        score_stage(2 * k + 2, sa_ref)
        softmax_stage(2 * k + 1, sb_ref)
        return carry

    score_stage(0, sa_ref)
    lax.fori_loop(0, (nstage + 1) // 2, body, 0)
    o_s = acc_ref[SELECTED] * (1.0 / jnp.sum(l_ref[SELECTED], axis=0, keepdims=True))
    o_w = acc_ref[WINDOWED] * (1.0 / jnp.sum(l_ref[WINDOWED], axis=0, keepdims=True))

    gates = jax.nn.sigmoid(gl_ref[0, 0, ti])
    o = gates[0:1] * oc_ref[0, 0, ti].astype(F32) + gates[1:2] * o_s + gates[2:3] * o_w
    pieces = [jnp.concatenate([o[:, r * QB:(r + 1) * QB], o[:, (r + 1) * QB:(r + 2) * QB]], axis=0).T
              for r in range(0, NSA_R, 2)]
    o_ref[0, pl.ds(pl.multiple_of(ti * QB, QB), QB), :] = jnp.concatenate(pieces, axis=1).astype(o_ref.dtype)
    return carry


def tile_lists(pair_flags):
    nq = pair_flags.shape[-1]
    nwin = WINDOW // KT
    n = jnp.arange(nq, dtype=jnp.int32)[:, None]
    wslots = -(-(nwin + 1) // SEL_GROUP) * SEL_GROUP
    wp = n - jnp.arange(wslots, dtype=jnp.int32)[None, :]
    wcode = jnp.where((wp >= 0) & (wp >= n - nwin), (WINDOWED << CODE_SHIFT) + wp, -1)
    p = jnp.arange(nq, dtype=jnp.int32)[None, :]
    active = ((pair_flags != 0) & (p < n)) | (p == n)
    order = jnp.sort(jnp.where(active, p, nq), axis=-1)
    sslots = -(-nq // SEL_GROUP) * SEL_GROUP
    order = jnp.pad(order, ((0, 0),) * 3 + ((0, sslots - nq),), constant_values=nq)
    scode = jnp.where(order < nq, (SELECTED << CODE_SHIFT) + order, -1)
    lead = scode.shape[:3]
    codes = jnp.concatenate([jnp.broadcast_to(wcode, lead + (wslots,)), scode,
                             jnp.full(lead + (2 * SEL_GROUP,), -1, jnp.int32)], axis=-1).astype(jnp.int32)
    cnt = jnp.sum(active, axis=-1).astype(jnp.int32)
    stages = wslots // SEL_GROUP + (cnt + SEL_GROUP - 1) // SEL_GROUP
    return codes.reshape(-1), stages.reshape(-1), codes.shape[-1]


def sel_win_attention(codes, stages, stride, q_t, kk, vv, sel, o_c, gl, slope_row, bias):
    b, g, nq, d, _ = q_t.shape
    t = kk.shape[3]
    ns = sel.shape[3]
    assert nq < (1 << CODE_SHIFT)
    tps = TILES_PER_STEP
    tile = lambda shape: pl.BlockSpec((1, 1, tps) + shape, lambda i, j, n, c, s: (i, j, n, 0, 0))
    grid_spec = pltpu.PrefetchScalarGridSpec(
        num_scalar_prefetch=2,
        grid=(b, g, nq // tps),
        in_specs=[
            tile((d, HQ)),
            pl.BlockSpec((1, 1, 2, t, d), lambda i, j, n, c, s: (i, j, 0, 0, 0)),
            pl.BlockSpec((1, 1, 2, t // KT, d, KT), lambda i, j, n, c, s: (i, j, 0, 0, 0, 0)),
            tile((ns, QB)), tile((d, HQ)), tile((3, HQ)),
            pl.BlockSpec((1, 1, HQ), lambda i, j, n, c, s: (j, 0, 0)),
            pl.BlockSpec((1, 3, KT, HQ), lambda i, j, n, c, s: (j, 0, 0, 0)),
        ],
        out_specs=pl.BlockSpec((1, tps * QB, NSA_R * d), lambda i, j, n, c, s: (i, n, j)),
        scratch_shapes=[pltpu.VMEM((2, 1, HQ), F32), pltpu.VMEM((2, SUBL, HQ), F32), pltpu.VMEM((2, d, HQ), F32),
                        pltpu.VMEM((SEL_GROUP, KT, HQ), F32), pltpu.VMEM((SEL_GROUP, KT, HQ), F32)],
    )
    return pl.pallas_call(
        functools.partial(_sel_win_kernel, nq=nq, stride=stride),
        grid_spec=grid_spec,
        out_shape=jax.ShapeDtypeStruct((b, t, g * NSA_R * d), BF16),
        compiler_params=_cparams(("parallel", "parallel", "arbitrary")),
    )(codes, stages, q_t, kk, vv, sel, o_c, gl, slope_row, bias)


def _row_tile(m):
    return 512 if m % 512 == 0 else m


def even_layer_heads(h, norm_g, w_in, lb, hgrn_norm, ret_norm):
    b, t, dm = h.shape
    proj = norm_matmul(h.reshape(b * t, dm), norm_g, w_in.astype(BF16), _row_tile(b * t), 1024, BF16)
    o = even_mix(proj.reshape(b, t, EVEN_IN), lb, hgrn_norm, ret_norm, min(t, 512))
    return o.reshape(b * t, -1)


def odd_layer_heads(h, norm_g, w_in, cmp_pos_k, cmp_pos_v, cmp_w1_k, cmp_w2_k, cmp_w1_v, cmp_w2_v):
    b, t, dm = h.shape
    g_, r_, d = NSA_G, NSA_R, NSA_HD
    nq = t // QB
    ns = t // SLC_BLOCK
    nch = t // CMP_STRIDE
    q_t, kc, vc, kk, vv, gl_t = odd_project(h, norm_g, w_in, min(t, 512), (d ** -0.5) * LOG2E)
    chunks = lambda a: a.reshape(b, g_, nch, CMP_STRIDE * d)
    k_cmp, v_cmp = compress(chunks(kc), chunks(vc), cmp_pos_k, cmp_pos_v,
                            cmp_w1_k.astype(BF16), cmp_w1_v.astype(BF16),
                            cmp_w2_k.astype(BF16), cmp_w2_v.astype(BF16))
    v_cmp_t = v_cmp.transpose(0, 1, 3, 2)

    slopes = jnp.exp2(-8.0 * jnp.arange(1, NSA_HEADS + 1, dtype=F32) / NSA_HEADS).reshape(g_, r_)
    slope_row = jnp.repeat(slopes, QB, axis=1).reshape(g_, 1, HQ) * LOG2E
    tq = np.tile(np.arange(QB), r_)[None, :]
    c_rel = np.arange(-(QB // CMP_STRIDE) * (nq - 1), nch)[:, None]
    dist_rel = tq - (c_rel * CMP_STRIDE + CMP_BLOCK - 1)
    cmp_bias = (slope_row * jnp.asarray(dist_rel, F32)[None]
                + jnp.asarray(np.where(dist_rel >= 0, 0.0, -NEG_INF), F32)[None])
    rel = tq - np.arange(KT)[:, None]
    mid = slope_row * jnp.asarray(rel, F32)[None]
    bias = jnp.stack([mid, mid + jnp.asarray(np.where(rel >= 0, 0.0, -NEG_INF), F32),
                      mid + jnp.asarray(np.where(rel < 0, 0.0, -NEG_INF), F32)], axis=1)
    c0 = np.arange(nch) * CMP_STRIDE
    s0 = np.arange(ns) * SLC_BLOCK
    ov = np.maximum(np.minimum(c0[:, None] + CMP_BLOCK, s0[None, :] + SLC_BLOCK)
                    - np.maximum(c0[:, None], s0[None, :]), 0) / CMP_BLOCK
    ov[nch - (CMP_BLOCK // CMP_STRIDE - 1):] = 0.0
    ovt = jnp.asarray(ov.T, BF16)

    o_c, sel, flags = cmp_attention(q_t, k_cmp, v_cmp_t, cmp_bias, ovt)
    codes, stages, stride = tile_lists(flags.reshape(b, g_, nq, ns // 2, 2).max(axis=-1))
    o = sel_win_attention(codes, stages, stride, q_t, kk, vv, sel, o_c, gl_t, slope_row, bias)
    return o.reshape(b * t, NSA_HEADS * d)


def finish_layer(h, heads, w_out, norm_g, w_gate_up, w_down, final_g):
    b, t, dm = h.shape
    m = b * t
    out = mixer_out_ffn(h.reshape(m, dm), heads, w_out.astype(BF16), norm_g, w_gate_up.astype(BF16),
                        w_down.astype(BF16), final_g, 1024 if m % 1024 == 0 else m, 256)
    return out.reshape(b, t, dm)


def kernel(x, mix_norm, ffn_norm, final_norm, even_w_in, hgrn_lower_bounds, hgrn_out_norm, ret_out_norm,
           even_w_out, odd_w_in, cmp_pos_k, cmp_pos_v, cmp_w1_k, cmp_w2_k, cmp_w1_v, cmp_w2_v, odd_w_out,
           ffn_w_gate_up, ffn_w_down):
    depth = mix_norm.shape[0]
    lb_all = jnp.cumsum(jax.nn.softmax(hgrn_lower_bounds.astype(F32), axis=0), axis=0)
    h = x
    for layer in range(depth):
        if layer % 2 == 0:
            e = layer // 2
            heads = even_layer_heads(h, mix_norm[layer], even_w_in[e], lb_all[e], hgrn_out_norm[e], ret_out_norm[e])
            w_out = even_w_out[e]
        else:
            o = layer // 2
            heads = odd_layer_heads(h, mix_norm[layer], odd_w_in[o], cmp_pos_k[o], cmp_pos_v[o], cmp_w1_k[o],
                                    cmp_w2_k[o], cmp_w1_v[o], cmp_w2_v[o])
            w_out = odd_w_out[o]
        h = finish_layer(h, heads, w_out, ffn_norm[layer], ffn_w_gate_up[layer], ffn_w_down[layer],
                         final_norm if layer == depth - 1 else None)
    return h
```

```python
import functools

import numpy as np
import jax
import jax.numpy as jnp
from jax import lax
from jax.experimental import pallas as pl
from jax.experimental.pallas import tpu as pltpu

F32 = jnp.float32
BF16 = jnp.bfloat16

D_MODEL = 1024
RMS_EPS = 1e-6
NEG_INF = -1e30
FORCE_SCORE = 1e9

LIN_HEADS = 4
LIN_D = 128
LIN_CHUNK = 64
SUB = 16
EVEN_IN = 8 * LIN_HEADS * LIN_D

NSA_HD = 64
NSA_HEADS = 16
NSA_G = 2
NSA_R = NSA_HEADS // NSA_G
CMP_BLOCK = 32
CMP_STRIDE = 16
SLC_BLOCK = 64
SLC_TOPN = 16
WINDOW = 512
QB = 128
KT = 128
CT = 128
TILES_PER_STEP = 8
SUBL = 8
SELECTED, WINDOWED = 0, 1
CODE_SHIFT = 10
HQ = NSA_R * QB
LOG2E = 1.4426950408889634

FFN_HIDDEN = 2816

VMEM_LIMIT = 56 * 1024 * 1024

_NT = (((1,), (1,)), ((), ()))
_TN = (((0,), (0,)), ((), ()))


def _cparams(sem):
    return pltpu.CompilerParams(dimension_semantics=sem, vmem_limit_bytes=VMEM_LIMIT)


def _rms(x, g):
    ms = jnp.mean(x * x, axis=-1, keepdims=True)
    return x * lax.rsqrt(ms + RMS_EPS) * g


def _silu(x):
    return x * jax.nn.sigmoid(x)


def _norm_matmul_kernel(x_ref, g_ref, w_ref, o_ref, *, tn):
    xn = _rms(x_ref[...], g_ref[...]).astype(BF16)
    for c in range(w_ref.shape[1] // tn):
        cols = slice(c * tn, (c + 1) * tn)
        o_ref[:, cols] = jnp.dot(xn, w_ref[:, cols], preferred_element_type=F32).astype(o_ref.dtype)


def norm_matmul(x, g, w, tm, tn, out_dtype):
    m, k = x.shape
    n = w.shape[1]
    return pl.pallas_call(
        functools.partial(_norm_matmul_kernel, tn=tn),
        grid=(m // tm,),
        in_specs=[
            pl.BlockSpec((tm, k), lambda i: (i, 0)),
            pl.BlockSpec((1, k), lambda i: (0, 0)),
            pl.BlockSpec((k, n), lambda i: (0, 0)),
        ],
        out_specs=pl.BlockSpec((tm, n), lambda i: (i, 0)),
        out_shape=jax.ShapeDtypeStruct((m, n), out_dtype),
        compiler_params=_cparams(("parallel",)),
    )(x, g.reshape(1, k), w)


def _ffn_kernel(x_ref, a_ref, wo_ref, g_ref, wgu_ref, wd_ref, fg_ref, o_ref, xn_ref, acc_ref, *, final_norm):
    j = pl.program_id(1)
    th = wd_ref.shape[0]

    @pl.when(j == 0)
    def _():
        h = x_ref[...] + jnp.dot(a_ref[...], wo_ref[...], preferred_element_type=F32)
        xn_ref[...] = _rms(h, g_ref[...]).astype(BF16)
        acc_ref[...] = h

    xn = xn_ref[...]
    gate = jnp.dot(xn, wgu_ref[0, :, :th], preferred_element_type=F32)
    up = jnp.dot(xn, wgu_ref[0, :, th:], preferred_element_type=F32)
    a = (_silu(gate) * up).astype(BF16)
    acc_ref[...] += jnp.dot(a, wd_ref[...], preferred_element_type=F32)

    @pl.when(j == pl.num_programs(1) - 1)
    def _():
        h = acc_ref[...]
        if final_norm:
            h = _rms(h, fg_ref[...])
        o_ref[...] = h


def mixer_out_ffn(x, a, w_out, g, w_gate_up, w_down, final_g, tm, th):
    m, k = x.shape
    hid = w_down.shape[0]
    nj = hid // th
    final_norm = final_g is not None
    fg = final_g if final_norm else g
    w_gu = w_gate_up.reshape(k, 2, nj, th).transpose(2, 0, 1, 3).reshape(nj, k, 2 * th)
    return pl.pallas_call(
        functools.partial(_ffn_kernel, final_norm=final_norm),
        grid=(m // tm, nj),
        in_specs=[
            pl.BlockSpec((tm, k), lambda i, j: (i, 0)),
            pl.BlockSpec((tm, k), lambda i, j: (i, 0)),
            pl.BlockSpec((k, k), lambda i, j: (0, 0)),
            pl.BlockSpec((1, k), lambda i, j: (0, 0)),
            pl.BlockSpec((1, k, 2 * th), lambda i, j: (j, 0, 0)),
            pl.BlockSpec((th, k), lambda i, j: (j, 0)),
            pl.BlockSpec((1, k), lambda i, j: (0, 0)),
        ],
        out_specs=pl.BlockSpec((tm, k), lambda i, j: (i, 0)),
        out_shape=jax.ShapeDtypeStruct((m, k), F32),
        scratch_shapes=[pltpu.VMEM((tm, k), BF16), pltpu.VMEM((tm, k), F32)],
        compiler_params=_cparams(("parallel", "arbitrary")),
    )(x, a, w_out, g.reshape(1, k), w_gu, w_down, fg.reshape(1, k))


def _cumsum_rows(g):
    c = g.shape[0]
    row = lax.broadcasted_iota(jnp.int32, g.shape, 0)
    s = 1
    while s < c:
        g = g + jnp.where(row >= s, pltpu.roll(g, s, axis=0), 0.0)
        s *= 2
    return g


def _hgrn_chunk(hq, hf, hi, lb, st_ref, h):
    c = LIN_CHUNK
    f = lb + (1.0 - lb) * jax.nn.sigmoid(hf)
    k = 1.0 - f
    q = _silu(hq)
    v = hi
    cum = _cumsum_rows(jnp.log2(f))
    st = st_ref[h]
    o_inter = lax.dot_general((q * jnp.exp2(cum)).astype(BF16), st.astype(BF16), _NT,
                              preferred_element_type=F32)
    irow = lax.broadcasted_iota(jnp.int32, (SUB, LIN_D), 0)
    krow = lax.broadcasted_iota(jnp.int32, (c, LIN_D), 0)
    lane = lax.broadcasted_iota(jnp.int32, (SUB, c), 1)
    vb = v.astype(BF16)
    attn_rows = []
    for a in range(c // SUB):
        lo = a * SUB
        cum_a = cum[lo:lo + SUB]
        q_a = q[lo:lo + SUB]
        k_a = k[lo:lo + SUB]
        if a > 0:
            ref = cum[lo - 1:lo, :]
            qd = (q_a * jnp.exp2(cum_a - ref)).astype(BF16)
            kd = (k * jnp.exp2(jnp.where(krow < lo, ref - cum, NEG_INF))).astype(BF16)
            attn = lax.dot_general(qd, kd, _NT, preferred_element_type=F32)
        else:
            attn = jnp.zeros((SUB, c), F32)
        for j in range(SUB):
            dec = jnp.exp2(jnp.where(irow >= j, cum_a - cum_a[j:j + 1, :], NEG_INF))
            w = (q_a * k_a[j:j + 1, :]) * dec
            attn = jnp.where(lane == lo + j, jnp.sum(w, axis=-1, keepdims=True), attn)
        attn_rows.append(attn)
    attn = jnp.concatenate(attn_rows, axis=0).astype(BF16)
    o_intra = jnp.dot(attn, vb, preferred_element_type=F32)
    last = cum[c - 1:c, :]
    kd = (k * jnp.exp2(last - cum)).astype(BF16)
    st_ref[h] = st * jnp.exp2(last) + lax.dot_general(vb, kd, _TN, preferred_element_type=F32)
    return o_intra + o_inter


def _ret_chunk(rq, rk, rv, dec, qdec, kdec, cdec, st_ref, h):
    k = rk * (LIN_D ** -0.5)
    st = st_ref[h]
    qb = rq.astype(BF16)
    vb = rv.astype(BF16)
    attn = lax.dot_general(qb, k.astype(BF16), _NT, preferred_element_type=F32) * dec
    o = jnp.dot(attn.astype(BF16), vb, preferred_element_type=F32)
    o = o + lax.dot_general((rq * qdec).astype(BF16), st.astype(BF16), _NT, preferred_element_type=F32)
    st_ref[h] = st * cdec + lax.dot_general(vb, (k * kdec).astype(BF16), _TN, preferred_element_type=F32)
    return o


def _even_mix_kernel(p_ref, lb_ref, hn_ref, rn_ref, dec_ref, qdec_ref, kdec_ref, cdec_ref,
                     o_ref, st_ref, *, tb):
    @pl.when(pl.program_id(1) == 0)
    def _():
        st_ref[...] = jnp.zeros_like(st_ref)

    w = LIN_HEADS * LIN_D

    def chunk(ci, carry):
        rows = pl.ds(pl.multiple_of(ci * LIN_CHUNK, LIN_CHUNK), LIN_CHUNK)
        for h in range(LIN_HEADS):
            col = lambda part: p_ref[0, rows, part * w + h * LIN_D:part * w + (h + 1) * LIN_D].astype(F32)
            o = _hgrn_chunk(col(0), col(1), col(2), lb_ref[:, h * LIN_D:(h + 1) * LIN_D], st_ref, h)
            o = _rms(o, hn_ref[...]) * _silu(col(3))
            o_ref[0, rows, h * LIN_D:(h + 1) * LIN_D] = o.astype(o_ref.dtype)
        for h in range(LIN_HEADS):
            col = lambda part: p_ref[0, rows, (4 + part) * w + h * LIN_D:(4 + part) * w + (h + 1) * LIN_D].astype(F32)
            o = _ret_chunk(col(0), col(1), col(2), dec_ref[h], qdec_ref[h], kdec_ref[h], cdec_ref[h],
                           st_ref, LIN_HEADS + h)
            o = _rms(o, rn_ref[...]) * _silu(col(3))
            o_ref[0, rows, w + h * LIN_D:w + (h + 1) * LIN_D] = o.astype(o_ref.dtype)
        return carry

    lax.fori_loop(0, tb // LIN_CHUNK, chunk, 0)


def even_mix(proj, lb, hgrn_norm, ret_norm, tb):
    b, t, _ = proj.shape
    c = LIN_CHUNK
    log_gamma = jnp.log(1.0 - jnp.exp2(-5.0 - jnp.arange(LIN_HEADS, dtype=F32)))
    pos = jnp.arange(c, dtype=F32)
    rel = pos[:, None] - pos[None, :]
    dec = jnp.where(rel[None] >= 0, jnp.exp(jnp.maximum(rel, 0.0)[None] * log_gamma[:, None, None]), 0.0)
    ones = jnp.ones((1, 1, LIN_D), F32)
    qdec = jnp.exp((pos + 1.0)[None, :] * log_gamma[:, None])[..., None] * ones
    kdec = jnp.exp((c - 1.0 - pos)[None, :] * log_gamma[:, None])[..., None] * ones
    cdec = jnp.exp(c * log_gamma)[:, None, None] * ones
    const = lambda shape: pl.BlockSpec(shape, lambda i, j: (0,) * len(shape))
    return pl.pallas_call(
        functools.partial(_even_mix_kernel, tb=tb),
        grid=(b, t // tb),
        in_specs=[
            pl.BlockSpec((1, tb, EVEN_IN), lambda i, j: (i, j, 0)),
            const((1, LIN_HEADS * LIN_D)),
            const((1, LIN_D)),
            const((1, LIN_D)),
            const((LIN_HEADS, c, c)),
            const((LIN_HEADS, c, LIN_D)),
            const((LIN_HEADS, c, LIN_D)),
            const((LIN_HEADS, 1, LIN_D)),
        ],
        out_specs=pl.BlockSpec((1, tb, 2 * LIN_HEADS * LIN_D), lambda i, j: (i, j, 0)),
        out_shape=jax.ShapeDtypeStruct((b, t, 2 * LIN_HEADS * LIN_D), BF16),
        scratch_shapes=[pltpu.VMEM((2 * LIN_HEADS, LIN_D, LIN_D), F32)],
        compiler_params=_cparams(("parallel", "arbitrary")),
    )(proj, lb.reshape(1, -1), hgrn_norm.reshape(1, -1), ret_norm.reshape(1, -1), dec, qdec, kdec, cdec)


def _odd_proj_kernel(x_ref, g_ref, wq_ref, wn_ref, wv_ref, wg_ref,
                     q_ref, kc_ref, vc_ref, kk_ref, vv_ref, gl_ref, *, q_scale):
    g_, r_, d = NSA_G, NSA_R, NSA_HD
    xn = _rms(x_ref[0], g_ref[...]).astype(BF16)
    tm = xn.shape[0]
    qt = (lax.dot_general(wq_ref[...], xn, _NT, preferred_element_type=F32) * q_scale).astype(BF16)
    for g in range(g_):
        for r in range(r_):
            rows = slice((g * r_ + r) * d, (g * r_ + r + 1) * d)
            for j in range(tm // QB):
                q_ref[0, g, j, :, r * QB:(r + 1) * QB] = qt[rows, j * QB:(j + 1) * QB]
    nat = jnp.dot(xn, wn_ref[...], preferred_element_type=F32)
    col = lambda i, g: nat[:, (i * g_ + g) * d:(i * g_ + g + 1) * d]
    for g in range(g_):
        kc_ref[0, g] = col(0, g)
        vc_ref[0, g] = col(1, g)
        kk_ref[0, g, SELECTED] = col(2, g).astype(BF16)
        kk_ref[0, g, WINDOWED] = col(3, g).astype(BF16)
    vt = lax.dot_general(wv_ref[...], xn, _NT, preferred_element_type=F32).astype(BF16)
    for i in (SELECTED, WINDOWED):
        for g in range(g_):
            for j in range(tm // KT):
                vv_ref[0, g, i, j] = vt[(i * g_ + g) * d:(i * g_ + g + 1) * d, j * KT:(j + 1) * KT]
    glt = lax.dot_general(wg_ref[...], xn, _NT, preferred_element_type=F32)
    for g in range(g_):
        for br in range(3):
            for r in range(r_):
                row = (g * 3 + br) * r_ + r
                for j in range(tm // QB):
                    gl_ref[0, g, j, br:br + 1, r * QB:(r + 1) * QB] = glt[row:row + 1, j * QB:(j + 1) * QB]


def odd_project(h, norm_g, w_in, tm, q_scale):
    b, t, dm = h.shape
    g_, r_, d = NSA_G, NSA_R, NSA_HD
    kvw = g_ * d
    o0 = NSA_HEADS * d
    part = lambda i: w_in[:, o0 + i * kvw:o0 + (i + 1) * kvw]
    wq = w_in[:, :o0].T.astype(BF16)
    wn = jnp.concatenate([part(0), part(1), part(2), part(4)], axis=1).astype(BF16)
    wv = jnp.concatenate([part(3), part(5)], axis=1).T.astype(BF16)
    wg = w_in[:, o0 + 6 * kvw:o0 + 6 * kvw + 3 * NSA_HEADS]
    wg = wg.reshape(dm, g_, r_, 3).transpose(1, 3, 2, 0).reshape(g_ * 3 * r_, dm).astype(BF16)
    const = lambda a: pl.BlockSpec(a.shape, lambda i, j: (0,) * a.ndim)
    nqt, nkt = tm // QB, tm // KT
    tok = pl.BlockSpec((1, g_, tm, d), lambda i, j: (i, 0, j, 0))
    tok_shape = jax.ShapeDtypeStruct((b, g_, t, d), F32)
    kk_spec = pl.BlockSpec((1, g_, 2, tm, d), lambda i, j: (i, 0, 0, j, 0))
    kk_shape = jax.ShapeDtypeStruct((b, g_, 2, t, d), BF16)
    vv_spec = pl.BlockSpec((1, g_, 2, nkt, d, KT), lambda i, j: (i, 0, 0, j, 0, 0))
    vv_shape = jax.ShapeDtypeStruct((b, g_, 2, t // KT, d, KT), BF16)
    return pl.pallas_call(
        functools.partial(_odd_proj_kernel, q_scale=q_scale),
        grid=(b, t // tm),
        in_specs=[pl.BlockSpec((1, tm, dm), lambda i, j: (i, j, 0)), pl.BlockSpec((1, dm), lambda i, j: (0, 0)),
                  const(wq), const(wn), const(wv), const(wg)],
        out_specs=[pl.BlockSpec((1, g_, nqt, d, HQ), lambda i, j: (i, 0, j, 0, 0)),
                   tok, tok, kk_spec, vv_spec,
                   pl.BlockSpec((1, g_, nqt, 3, HQ), lambda i, j: (i, 0, j, 0, 0))],
        out_shape=[jax.ShapeDtypeStruct((b, g_, t // QB, d, HQ), BF16),
                   tok_shape, tok_shape, kk_shape, vv_shape,
                   jax.ShapeDtypeStruct((b, g_, t // QB, 3, HQ), F32)],
        compiler_params=_cparams(("parallel", "parallel")),
    )(h, norm_g.reshape(1, dm), wq, wn, wv, wg)


def _compress_kernel(ck_ref, cv_ref, pk_ref, pv_ref, w1k_ref, w1v_ref, w2k_ref, w2v_ref, ok_ref, ov_ref):
    half = CMP_STRIDE * NSA_HD

    def one(c_ref, p_ref, w1_ref, w2_ref):
        ch = c_ref[0, 0]
        a = jnp.dot((ch + p_ref[:, :half]).astype(BF16), w1_ref[:half, :], preferred_element_type=F32)
        b = jnp.dot((ch + p_ref[:, half:]).astype(BF16), w1_ref[half:, :], preferred_element_type=F32)
        nch = a.shape[0]
        pre = a + pltpu.roll(b, nch - 1, axis=0)
        return jnp.dot(_silu(pre).astype(BF16), w2_ref[...], preferred_element_type=F32)

    ok_ref[0, 0] = one(ck_ref, pk_ref, w1k_ref, w2k_ref).astype(ok_ref.dtype)
    ov_ref[0, 0] = one(cv_ref, pv_ref, w1v_ref, w2v_ref).astype(ov_ref.dtype)


def compress(ck, cv, pos_k, pos_v, w1k, w1v, w2k, w2v):
    b, g, nch, width = ck.shape
    blk = pl.BlockSpec((1, 1, nch, width), lambda i, j: (i, j, 0, 0))
    const = lambda shape: pl.BlockSpec(shape, lambda i, j: (0,) * len(shape))
    oblk = pl.BlockSpec((1, 1, nch, NSA_HD), lambda i, j: (i, j, 0, 0))
    oshape = jax.ShapeDtypeStruct((b, g, nch, NSA_HD), BF16)
    return pl.pallas_call(
        _compress_kernel,
        grid=(b, g),
        in_specs=[blk, blk, const((1, 2 * width)), const((1, 2 * width)),
                  const((2 * width, NSA_HD)), const((2 * width, NSA_HD)),
                  const((NSA_HD, NSA_HD)), const((NSA_HD, NSA_HD))],
        out_specs=[oblk, oblk],
        out_shape=[oshape, oshape],
        compiler_params=_cparams(("parallel", "parallel")),
    )(ck, cv, pos_k.reshape(1, -1), pos_v.reshape(1, -1), w1k, w1v, w2k, w2v)


def _cmp_attn_kernel(q_ref, kc_ref, vc_ref, bias_ref, ovt_ref, oc_ref, sel_ref, flag_ref, pslc_ref, *, nq):
    nc = kc_ref.shape[2]
    ns = ovt_ref.shape[0]
    tps = q_ref.shape[2]
    lax.fori_loop(0, tps // 2, functools.partial(_cmp_attn_pair, q_ref, kc_ref, vc_ref, bias_ref, ovt_ref, oc_ref,
                                                 sel_ref, flag_ref, pslc_ref, nq, nc, ns, tps), 0)


def _cmp_attn_pair(q_ref, kc_ref, vc_ref, bias_ref, ovt_ref, oc_ref, sel_ref, flag_ref, pslc_ref,
                   nq, nc, ns, tps, pi, carry):
    n0 = pl.program_id(2) * tps + 2 * pi
    for u in (0, 1):
        _cmp_attend(q_ref, kc_ref, vc_ref, bias_ref, ovt_ref, oc_ref, pslc_ref, nq, nc, 2 * pi + u, n0 + u, u)
    pslc = pslc_ref[...]

    jj = lax.broadcasted_iota(jnp.int32, (2, ns, QB), 1)
    tile = lax.broadcasted_iota(jnp.int32, (2, ns, QB), 0)
    qblk = ((n0 + tile) * QB + lax.broadcasted_iota(jnp.int32, (2, ns, QB), 2)) >> 6
    forced = (jj == 0) | (jj == qblk) | (jj == qblk - 1)
    score = jnp.where(forced, -jnp.inf, jnp.where(jj <= qblk, pslc, NEG_INF))
    sel = jnp.where(forced, 1.0, 0.0)

    def pick_one(score, sel):
        mx = jnp.max(score, axis=1, keepdims=True)
        idx = jnp.min(jnp.where(score == mx, jj, ns), axis=1, keepdims=True)
        pick = jj == idx
        return jnp.where(pick, -jnp.inf, score), jnp.where(pick, 1.0, sel)

    for _ in range(max(min(SLC_TOPN, ns) - 3, 0)):
        score, sel = pick_one(score, sel)
    score, sel = lax.fori_loop(0, jnp.where(n0 == 0, 2, 0), lambda _, c: pick_one(*c), (score, sel))
    for u in (0, 1):
        sel_ref[0, 0, 2 * pi + u] = sel[u]
        flag_ref[0, 0, 2 * pi + u] = jnp.max(sel[u], axis=1, keepdims=True).astype(jnp.int32)
    return carry


def _cmp_attend(q_ref, kc_ref, vc_ref, bias_ref, ovt_ref, oc_ref, pslc_ref, nq, nc, ti, n, slot):
    qt = q_ref[0, 0, ti]
    row0 = pl.multiple_of((QB // CMP_STRIDE) * (nq - 1 - n), QB // CMP_STRIDE)
    tiles_needed = ((QB // CMP_STRIDE) * n + (QB - CMP_BLOCK) // CMP_STRIDE) // CT + 1

    def attend(rows):
        s = jnp.dot(kc_ref[0, 0, :rows, :], qt, preferred_element_type=F32) - bias_ref[0, pl.ds(row0, rows), :]
        m = jnp.maximum(jnp.max(s, axis=0, keepdims=True), 0.1 * NEG_INF)
        e = jnp.exp2(s - m)
        l = jnp.sum(e, axis=0, keepdims=True)
        p = e * (1.0 / jnp.where(l > 0, l, 1.0))
        oc_ref[0, 0, ti] = jnp.dot(vc_ref[0, 0, :, :rows], p.astype(BF16),
                                   preferred_element_type=F32).astype(oc_ref.dtype)
        psum = p[:, 0:QB]
        for r in range(1, NSA_R):
            psum = psum + p[:, r * QB:(r + 1) * QB]
        hi = psum.astype(BF16)
        lo = (psum - hi.astype(F32)).astype(BF16)
        ovt = ovt_ref[:, :rows]
        pslc_ref[slot] = (jnp.dot(ovt, hi, preferred_element_type=F32)
                          + jnp.dot(ovt, lo, preferred_element_type=F32))

    for v in range(1, nc // CT + 1):
        pl.when(tiles_needed == v)(functools.partial(attend, v * CT))


def cmp_attention(q_t, k_cmp, v_cmp_t, cmp_bias, ovt):
    b, g, nq, d, _ = q_t.shape
    nc = k_cmp.shape[2]
    ns = ovt.shape[0]
    assert nc % CT == 0 and cmp_bias.shape[1] == (QB // CMP_STRIDE) * (nq - 1) + nc
    assert nq % TILES_PER_STEP == 0 and TILES_PER_STEP % 2 == 0
    tile = lambda shape: pl.BlockSpec((1, 1, TILES_PER_STEP) + shape, lambda i, j, n: (i, j, n, 0, 0))
    return pl.pallas_call(
        functools.partial(_cmp_attn_kernel, nq=nq),
        grid=(b, g, nq // TILES_PER_STEP),
        in_specs=[
            tile((d, HQ)),
            pl.BlockSpec((1, 1, nc, d), lambda i, j, n: (i, j, 0, 0)),
            pl.BlockSpec((1, 1, d, nc), lambda i, j, n: (i, j, 0, 0)),
            pl.BlockSpec((1,) + cmp_bias.shape[1:], lambda i, j, n: (j, 0, 0)),
            pl.BlockSpec((ns, nc), lambda i, j, n: (0, 0)),
        ],
        out_specs=[tile((d, HQ)), tile((ns, QB)), tile((ns, 1))],
        out_shape=[
            jax.ShapeDtypeStruct((b, g, nq, d, HQ), BF16),
            jax.ShapeDtypeStruct((b, g, nq, ns, QB), F32),
            jax.ShapeDtypeStruct((b, g, nq, ns, 1), jnp.int32),
        ],
        scratch_shapes=[pltpu.VMEM((2, ns, QB), F32)],
        compiler_params=_cparams(("parallel", "parallel", "arbitrary")),
    )(q_t, k_cmp, v_cmp_t, cmp_bias, ovt)


def _scores(qt, kb, bias):
    return jnp.dot(kb, qt, preferred_element_type=F32) - bias


def _flash_step(scores, tiles, state):
    half = SLC_BLOCK
    tile_max = None
    for s, (_, off, chosen) in zip(scores, tiles):
        if chosen is None:
            mx = jnp.max(s, axis=0, keepdims=True)
        else:
            mx = jnp.maximum(jnp.where(chosen[0] > 0, jnp.max(s[:half], axis=0, keepdims=True), NEG_INF),
                             jnp.where(chosen[1] > 0, jnp.max(s[half:], axis=0, keepdims=True), NEG_INF))
        tile_max = mx - off if tile_max is None else jnp.maximum(tile_max, mx - off)
    if state is None:
        m_new, l, acc = tile_max, 0.0, 0.0
    else:
        m_old, l_old, acc_old = state
        m_new = jnp.maximum(m_old, tile_max)
        alpha = jnp.exp2(m_old - m_new)
        l, acc = alpha * l_old, alpha * acc_old
    for s, (vtb, off, chosen) in zip(scores, tiles):
        c0 = c1 = m_new + off
        if chosen is not None:
            c0 = jnp.where(chosen[0] > 0, c0, -NEG_INF)
            c1 = jnp.where(chosen[1] > 0, c1, -NEG_INF)
        p0 = jnp.exp2(s[:half] - c0)
        p1 = jnp.exp2(s[half:] - c1)
        l = l + (p0.reshape(half // SUBL, SUBL, -1).sum(axis=0) + p1.reshape(half // SUBL, SUBL, -1).sum(axis=0))
        acc = acc + jnp.dot(vtb, jnp.concatenate([p0, p1], axis=0).astype(BF16), preferred_element_type=F32)
    return m_new, l, acc


BIAS_MID, BIAS_DIAG, BIAS_FIRST = 0, 1, 2


SEL_GROUP = 2


def _sel_win_kernel(lists_ref, nstage_ref, q_ref, kk_ref, vv_ref, sel_ref, oc_ref, gl_ref, slope_ref,
                    bias_ref, o_ref, m_ref, l_ref, acc_ref, sa_ref, sb_ref, *, nq, stride):
    tps = q_ref.shape[2]
    lax.fori_loop(0, tps, functools.partial(_sel_win_tile, lists_ref, nstage_ref, q_ref, kk_ref, vv_ref, sel_ref,
                                            oc_ref, gl_ref, slope_ref, bias_ref, o_ref, m_ref, l_ref, acc_ref,
                                            sa_ref, sb_ref, nq, stride, tps), 0)


def _sel_win_tile(lists_ref, nstage_ref, q_ref, kk_ref, vv_ref, sel_ref, oc_ref, gl_ref, slope_ref,
                  bias_ref, o_ref, m_ref, l_ref, acc_ref, sa_ref, sb_ref, nq, stride, tps, ti, carry):
    bi, gi, n = pl.program_id(0), pl.program_id(1), pl.program_id(2) * tps + ti
    qt = q_ref[0, 0, ti]
    slope = slope_ref[0]
    tile_id = (bi * NSA_G + gi) * nq + n
    base = tile_id * stride
    nstage = nstage_ref[tile_id]
    m_ref[...] = jnp.full(m_ref.shape, NEG_INF, F32)
    l_ref[...] = jnp.zeros(l_ref.shape, F32)
    acc_ref[...] = jnp.zeros(acc_ref.shape, F32)

    def entry(it, u):
        code = lists_ref[base + jnp.minimum(it, nstage) * SEL_GROUP + u]
        c = jnp.maximum(code, 0)
        return c >> CODE_SHIFT, c & ((1 << CODE_SHIFT) - 1), code >= 0

    def score_stage(it, s_ref):
        for u in range(SEL_GROUP):
            br, p, _ = entry(it, u)
            first = (br == WINDOWED) & (p == n - WINDOW // KT)
            kind = jnp.where(p == n, BIAS_DIAG, jnp.where(first, BIAS_FIRST, BIAS_MID))
            rows = pl.ds(pl.multiple_of(p * KT, KT), KT)
            s_ref[u] = _scores(qt, kk_ref[0, 0, br, rows, :], bias_ref[0, kind])

    def softmax_stage(it, s_ref):
        tiles = []
        for u in range(SEL_GROUP):
            br, p, live = entry(it, u)
            keep = jnp.where(live, 1.0, 0.0)
            everyone = jnp.where(br == WINDOWED, 1.0, 0.0)
            chosen = [keep * jnp.maximum(jnp.concatenate([sel_ref[0, 0, ti, pl.ds(2 * p + i, 1), :]] * NSA_R,
                                                         axis=1), everyone) for i in (0, 1)]
            tiles.append((vv_ref[0, 0, br, p], slope * ((n - p) * KT).astype(F32), chosen))
        sb, _, _ = entry(it, 0)
        m, l, acc = _flash_step([s_ref[u] for u in range(SEL_GROUP)], tiles, (m_ref[sb], l_ref[sb], acc_ref[sb]))
        m_ref[sb] = m
        l_ref[sb] = l
        acc_ref[sb] = acc

    def body(k, carry):
        score_stage(2 * k + 1, sb_ref)
        softmax_stage(2 * k, sa_ref)
        score_stage(2 * k + 2, sa_ref)
        softmax_stage(2 * k + 1, sb_ref)
        return carry

    score_stage(0, sa_ref)
    lax.fori_loop(0, (nstage + 1) // 2, body, 0)
    o_s = acc_ref[SELECTED] * (1.0 / jnp.sum(l_ref[SELECTED], axis=0, keepdims=True))
    o_w = acc_ref[WINDOWED] * (1.0 / jnp.sum(l_ref[WINDOWED], axis=0, keepdims=True))

    gates = jax.nn.sigmoid(gl_ref[0, 0, ti])
    o = gates[0:1] * oc_ref[0, 0, ti].astype(F32) + gates[1:2] * o_s + gates[2:3] * o_w
    pieces = [jnp.concatenate([o[:, r * QB:(r + 1) * QB], o[:, (r + 1) * QB:(r + 2) * QB]], axis=0).T
              for r in range(0, NSA_R, 2)]
    o_ref[0, pl.ds(pl.multiple_of(ti * QB, QB), QB), :] = jnp.concatenate(pieces, axis=1).astype(o_ref.dtype)
    return carry


def tile_lists(pair_flags):
    nq = pair_flags.shape[-1]
    nwin = WINDOW // KT
    n = jnp.arange(nq, dtype=jnp.int32)[:, None]
    wslots = -(-(nwin + 1) // SEL_GROUP) * SEL_GROUP
    wp = n - jnp.arange(wslots, dtype=jnp.int32)[None, :]
    wcode = jnp.where((wp >= 0) & (wp >= n - nwin), (WINDOWED << CODE_SHIFT) + wp, -1)
    p = jnp.arange(nq, dtype=jnp.int32)[None, :]
    active = ((pair_flags != 0) & (p < n)) | (p == n)
    order = jnp.sort(jnp.where(active, p, nq), axis=-1)
    sslots = -(-nq // SEL_GROUP) * SEL_GROUP
    order = jnp.pad(order, ((0, 0),) * 3 + ((0, sslots - nq),), constant_values=nq)
    scode = jnp.where(order < nq, (SELECTED << CODE_SHIFT) + order, -1)
    lead = scode.shape[:3]
    codes = jnp.concatenate([jnp.broadcast_to(wcode, lead + (wslots,)), scode,
                             jnp.full(lead + (2 * SEL_GROUP,), -1, jnp.int32)], axis=-1).astype(jnp.int32)
    cnt = jnp.sum(active, axis=-1).astype(jnp.int32)
    stages = wslots // SEL_GROUP + (cnt + SEL_GROUP - 1) // SEL_GROUP
    return codes.reshape(-1), stages.reshape(-1), codes.shape[-1]


def sel_win_attention(codes, stages, stride, q_t, kk, vv, sel, o_c, gl, slope_row, bias):
    b, g, nq, d, _ = q_t.shape
    t = kk.shape[3]
    ns = sel.shape[3]
    assert nq < (1 << CODE_SHIFT)
    tps = TILES_PER_STEP
    tile = lambda shape: pl.BlockSpec((1, 1, tps) + shape, lambda i, j, n, c, s: (i, j, n, 0, 0))
    grid_spec = pltpu.PrefetchScalarGridSpec(
        num_scalar_prefetch=2,
        grid=(b, g, nq // tps),
        in_specs=[
            tile((d, HQ)),
            pl.BlockSpec((1, 1, 2, t, d), lambda i, j, n, c, s: (i, j, 0, 0, 0)),
            pl.BlockSpec((1, 1, 2, t // KT, d, KT), lambda i, j, n, c, s: (i, j, 0, 0, 0, 0)),
            tile((ns, QB)), tile((d, HQ)), tile((3, HQ)),
            pl.BlockSpec((1, 1, HQ), lambda i, j, n, c, s: (j, 0, 0)),
            pl.BlockSpec((1, 3, KT, HQ), lambda i, j, n, c, s: (j, 0, 0, 0)),
        ],
        out_specs=pl.BlockSpec((1, tps * QB, NSA_R * d), lambda i, j, n, c, s: (i, n, j)),
        scratch_shapes=[pltpu.VMEM((2, 1, HQ), F32), pltpu.VMEM((2, SUBL, HQ), F32), pltpu.VMEM((2, d, HQ), F32),
                        pltpu.VMEM((SEL_GROUP, KT, HQ), F32), pltpu.VMEM((SEL_GROUP, KT, HQ), F32)],
    )
    return pl.pallas_call(
        functools.partial(_sel_win_kernel, nq=nq, stride=stride),
        grid_spec=grid_spec,
        out_shape=jax.ShapeDtypeStruct((b, t, g * NSA_R * d), BF16),
        compiler_params=_cparams(("parallel", "parallel", "arbitrary")),
    )(codes, stages, q_t, kk, vv, sel, o_c, gl, slope_row, bias)


def _row_tile(m):
    return 512 if m % 512 == 0 else m


def even_layer_heads(h, norm_g, w_in, lb, hgrn_norm, ret_norm):
    b, t, dm = h.shape
    proj = norm_matmul(h.reshape(b * t, dm), norm_g, w_in.astype(BF16), _row_tile(b * t), 1024, BF16)
    o = even_mix(proj.reshape(b, t, EVEN_IN), lb, hgrn_norm, ret_norm, min(t, 512))
    return o.reshape(b * t, -1)


def odd_layer_heads(h, norm_g, w_in, cmp_pos_k, cmp_pos_v, cmp_w1_k, cmp_w2_k, cmp_w1_v, cmp_w2_v):
    b, t, dm = h.shape
    g_, r_, d = NSA_G, NSA_R, NSA_HD
    nq = t // QB
    ns = t // SLC_BLOCK
    nch = t // CMP_STRIDE
    q_t, kc, vc, kk, vv, gl_t = odd_project(h, norm_g, w_in, min(t, 512), (d ** -0.5) * LOG2E)
    chunks = lambda a: a.reshape(b, g_, nch, CMP_STRIDE * d)
    k_cmp, v_cmp = compress(chunks(kc), chunks(vc), cmp_pos_k, cmp_pos_v,
                            cmp_w1_k.astype(BF16), cmp_w1_v.astype(BF16),
                            cmp_w2_k.astype(BF16), cmp_w2_v.astype(BF16))
    v_cmp_t = v_cmp.transpose(0, 1, 3, 2)

    slopes = jnp.exp2(-8.0 * jnp.arange(1, NSA_HEADS + 1, dtype=F32) / NSA_HEADS).reshape(g_, r_)
    slope_row = jnp.repeat(slopes, QB, axis=1).reshape(g_, 1, HQ) * LOG2E
    tq = np.tile(np.arange(QB), r_)[None, :]
    c_rel = np.arange(-(QB // CMP_STRIDE) * (nq - 1), nch)[:, None]
    dist_rel = tq - (c_rel * CMP_STRIDE + CMP_BLOCK - 1)
    cmp_bias = (slope_row * jnp.asarray(dist_rel, F32)[None]
                + jnp.asarray(np.where(dist_rel >= 0, 0.0, -NEG_INF), F32)[None])
    rel = tq - np.arange(KT)[:, None]
    mid = slope_row * jnp.asarray(rel, F32)[None]
    bias = jnp.stack([mid, mid + jnp.asarray(np.where(rel >= 0, 0.0, -NEG_INF), F32),
                      mid + jnp.asarray(np.where(rel < 0, 0.0, -NEG_INF), F32)], axis=1)
    c0 = np.arange(nch) * CMP_STRIDE
    s0 = np.arange(ns) * SLC_BLOCK
    ov = np.maximum(np.minimum(c0[:, None] + CMP_BLOCK, s0[None, :] + SLC_BLOCK)
                    - np.maximum(c0[:, None], s0[None, :]), 0) / CMP_BLOCK
    ov[nch - (CMP_BLOCK // CMP_STRIDE - 1):] = 0.0
    ovt = jnp.asarray(ov.T, BF16)

    o_c, sel, flags = cmp_attention(q_t, k_cmp, v_cmp_t, cmp_bias, ovt)
    codes, stages, stride = tile_lists(flags.reshape(b, g_, nq, ns // 2, 2).max(axis=-1))
    o = sel_win_attention(codes, stages, stride, q_t, kk, vv, sel, o_c, gl_t, slope_row, bias)
    return o.reshape(b * t, NSA_HEADS * d)


def finish_layer(h, heads, w_out, norm_g, w_gate_up, w_down, final_g):
    b, t, dm = h.shape
    m = b * t
    out = mixer_out_ffn(h.reshape(m, dm), heads, w_out.astype(BF16), norm_g, w_gate_up.astype(BF16),
                        w_down.astype(BF16), final_g, 1024 if m % 1024 == 0 else m, 256)
    return out.reshape(b, t, dm)


def kernel(x, mix_norm, ffn_norm, final_norm, even_w_in, hgrn_lower_bounds, hgrn_out_norm, ret_out_norm,
           even_w_out, odd_w_in, cmp_pos_k, cmp_pos_v, cmp_w1_k, cmp_w2_k, cmp_w1_v, cmp_w2_v, odd_w_out,
           ffn_w_gate_up, ffn_w_down):
    depth = mix_norm.shape[0]
    lb_all = jnp.cumsum(jax.nn.softmax(hgrn_lower_bounds.astype(F32), axis=0), axis=0)
    h = x
    for layer in range(depth):
        if layer % 2 == 0:
            e = layer // 2
            heads = even_layer_heads(h, mix_norm[layer], even_w_in[e], lb_all[e], hgrn_out_norm[e], ret_out_norm[e])
            w_out = even_w_out[e]
        else:
            o = layer // 2
            heads = odd_layer_heads(h, mix_norm[layer], odd_w_in[o], cmp_pos_k[o], cmp_pos_v[o], cmp_w1_k[o],
                                    cmp_w2_k[o], cmp_w1_v[o], cmp_w2_v[o])
            w_out = odd_w_out[o]
        h = finish_layer(h, heads, w_out, ffn_norm[layer], ffn_w_gate_up[layer], ffn_w_down[layer],
                         final_norm if layer == depth - 1 else None)
    return h
```

```python
import functools

import numpy as np
import jax
import jax.numpy as jnp
from jax import lax
from jax.experimental import pallas as pl
from jax.experimental.pallas import tpu as pltpu

F32 = jnp.float32
BF16 = jnp.bfloat16

D_MODEL = 1024
RMS_EPS = 1e-6
NEG_INF = -1e30
FORCE_SCORE = 1e9

LIN_HEADS = 4
LIN_D = 128
LIN_CHUNK = 64
SUB = 16
EVEN_IN = 8 * LIN_HEADS * LIN_D

NSA_HD = 64
NSA_HEADS = 16
NSA_G = 2
NSA_R = NSA_HEADS // NSA_G
CMP_BLOCK = 32
CMP_STRIDE = 16
SLC_BLOCK = 64
SLC_TOPN = 16
WINDOW = 512
QB = 128
KT = 128
CT = 128
TILES_PER_STEP = 8
SELECTED, WINDOWED = 0, 1
VROWS = NSA_HD + 16
CODE_SHIFT = 10
HQ = NSA_R * QB
LOG2E = 1.4426950408889634

FFN_HIDDEN = 2816

VMEM_LIMIT = 56 * 1024 * 1024

_NT = (((1,), (1,)), ((), ()))
_TN = (((0,), (0,)), ((), ()))


def _cparams(sem):
    return pltpu.CompilerParams(dimension_semantics=sem, vmem_limit_bytes=VMEM_LIMIT)


def _rms(x, g):
    ms = jnp.mean(x * x, axis=-1, keepdims=True)
    return x * lax.rsqrt(ms + RMS_EPS) * g


def _silu(x):
    return x * jax.nn.sigmoid(x)


def _norm_matmul_kernel(x_ref, g_ref, w_ref, o_ref, *, tn):
    xn = _rms(x_ref[...], g_ref[...]).astype(BF16)
    for c in range(w_ref.shape[1] // tn):
        cols = slice(c * tn, (c + 1) * tn)
        o_ref[:, cols] = jnp.dot(xn, w_ref[:, cols], preferred_element_type=F32).astype(o_ref.dtype)


def norm_matmul(x, g, w, tm, tn, out_dtype):
    m, k = x.shape
    n = w.shape[1]
    return pl.pallas_call(
        functools.partial(_norm_matmul_kernel, tn=tn),
        grid=(m // tm,),
        in_specs=[
            pl.BlockSpec((tm, k), lambda i: (i, 0)),
            pl.BlockSpec((1, k), lambda i: (0, 0)),
            pl.BlockSpec((k, n), lambda i: (0, 0)),
        ],
        out_specs=pl.BlockSpec((tm, n), lambda i: (i, 0)),
        out_shape=jax.ShapeDtypeStruct((m, n), out_dtype),
        compiler_params=_cparams(("parallel",)),
    )(x, g.reshape(1, k), w)


def _ffn_kernel(x_ref, a_ref, wo_ref, g_ref, wgu_ref, wd_ref, fg_ref, o_ref, xn_ref, acc_ref, *, final_norm):
    j = pl.program_id(1)
    th = wd_ref.shape[0]

    @pl.when(j == 0)
    def _():
        h = x_ref[...] + jnp.dot(a_ref[...], wo_ref[...], preferred_element_type=F32)
        xn_ref[...] = _rms(h, g_ref[...]).astype(BF16)
        acc_ref[...] = h

    xn = xn_ref[...]
    gate = jnp.dot(xn, wgu_ref[0, :, :th], preferred_element_type=F32)
    up = jnp.dot(xn, wgu_ref[0, :, th:], preferred_element_type=F32)
    a = (_silu(gate) * up).astype(BF16)
    acc_ref[...] += jnp.dot(a, wd_ref[...], preferred_element_type=F32)

    @pl.when(j == pl.num_programs(1) - 1)
    def _():
        h = acc_ref[...]
        if final_norm:
            h = _rms(h, fg_ref[...])
        o_ref[...] = h


def mixer_out_ffn(x, a, w_out, g, w_gate_up, w_down, final_g, tm, th):
    m, k = x.shape
    hid = w_down.shape[0]
    nj = hid // th
    final_norm = final_g is not None
    fg = final_g if final_norm else g
    w_gu = w_gate_up.reshape(k, 2, nj, th).transpose(2, 0, 1, 3).reshape(nj, k, 2 * th)
    return pl.pallas_call(
        functools.partial(_ffn_kernel, final_norm=final_norm),
        grid=(m // tm, nj),
        in_specs=[
            pl.BlockSpec((tm, k), lambda i, j: (i, 0)),
            pl.BlockSpec((tm, k), lambda i, j: (i, 0)),
            pl.BlockSpec((k, k), lambda i, j: (0, 0)),
            pl.BlockSpec((1, k), lambda i, j: (0, 0)),
            pl.BlockSpec((1, k, 2 * th), lambda i, j: (j, 0, 0)),
            pl.BlockSpec((th, k), lambda i, j: (j, 0)),
            pl.BlockSpec((1, k), lambda i, j: (0, 0)),
        ],
        out_specs=pl.BlockSpec((tm, k), lambda i, j: (i, 0)),
        out_shape=jax.ShapeDtypeStruct((m, k), F32),
        scratch_shapes=[pltpu.VMEM((tm, k), BF16), pltpu.VMEM((tm, k), F32)],
        compiler_params=_cparams(("parallel", "arbitrary")),
    )(x, a, w_out, g.reshape(1, k), w_gu, w_down, fg.reshape(1, k))


def _cumsum_rows(g):
    c = g.shape[0]
    row = lax.broadcasted_iota(jnp.int32, g.shape, 0)
    s = 1
    while s < c:
        g = g + jnp.where(row >= s, pltpu.roll(g, s, axis=0), 0.0)
        s *= 2
    return g


def _hgrn_chunk(hq, hf, hi, lb, st_ref, h):
    c = LIN_CHUNK
    f = lb + (1.0 - lb) * jax.nn.sigmoid(hf)
    k = 1.0 - f
    q = _silu(hq)
    v = hi
    cum = _cumsum_rows(jnp.log2(f))
    st = st_ref[h]
    o_inter = lax.dot_general((q * jnp.exp2(cum)).astype(BF16), st.astype(BF16), _NT,
                              preferred_element_type=F32)
    irow = lax.broadcasted_iota(jnp.int32, (SUB, LIN_D), 0)
    krow = lax.broadcasted_iota(jnp.int32, (c, LIN_D), 0)
    lane = lax.broadcasted_iota(jnp.int32, (SUB, c), 1)
    vb = v.astype(BF16)
    attn_rows = []
    for a in range(c // SUB):
        lo = a * SUB
        cum_a = cum[lo:lo + SUB]
        q_a = q[lo:lo + SUB]
        k_a = k[lo:lo + SUB]
        if a > 0:
            ref = cum[lo - 1:lo, :]
            qd = (q_a * jnp.exp2(cum_a - ref)).astype(BF16)
            kd = (k * jnp.exp2(jnp.where(krow < lo, ref - cum, NEG_INF))).astype(BF16)
            attn = lax.dot_general(qd, kd, _NT, preferred_element_type=F32)
        else:
            attn = jnp.zeros((SUB, c), F32)
        for j in range(SUB):
            dec = jnp.exp2(jnp.where(irow >= j, cum_a - cum_a[j:j + 1, :], NEG_INF))
            w = (q_a * k_a[j:j + 1, :]) * dec
            attn = jnp.where(lane == lo + j, jnp.sum(w, axis=-1, keepdims=True), attn)
        attn_rows.append(attn)
    attn = jnp.concatenate(attn_rows, axis=0).astype(BF16)
    o_intra = jnp.dot(attn, vb, preferred_element_type=F32)
    last = cum[c - 1:c, :]
    kd = (k * jnp.exp2(last - cum)).astype(BF16)
    st_ref[h] = st * jnp.exp2(last) + lax.dot_general(vb, kd, _TN, preferred_element_type=F32)
    return o_intra + o_inter


def _ret_chunk(rq, rk, rv, dec, qdec, kdec, cdec, st_ref, h):
    k = rk * (LIN_D ** -0.5)
    st = st_ref[h]
    qb = rq.astype(BF16)
    vb = rv.astype(BF16)
    attn = lax.dot_general(qb, k.astype(BF16), _NT, preferred_element_type=F32) * dec
    o = jnp.dot(attn.astype(BF16), vb, preferred_element_type=F32)
    o = o + lax.dot_general((rq * qdec).astype(BF16), st.astype(BF16), _NT, preferred_element_type=F32)
    st_ref[h] = st * cdec + lax.dot_general(vb, (k * kdec).astype(BF16), _TN, preferred_element_type=F32)
    return o


def _even_mix_kernel(p_ref, lb_ref, hn_ref, rn_ref, dec_ref, qdec_ref, kdec_ref, cdec_ref,
                     o_ref, st_ref, *, tb):
    @pl.when(pl.program_id(1) == 0)
    def _():
        st_ref[...] = jnp.zeros_like(st_ref)

    w = LIN_HEADS * LIN_D

    def chunk(ci, carry):
        rows = pl.ds(pl.multiple_of(ci * LIN_CHUNK, LIN_CHUNK), LIN_CHUNK)
        for h in range(LIN_HEADS):
            col = lambda part: p_ref[0, rows, part * w + h * LIN_D:part * w + (h + 1) * LIN_D].astype(F32)
            o = _hgrn_chunk(col(0), col(1), col(2), lb_ref[:, h * LIN_D:(h + 1) * LIN_D], st_ref, h)
            o = _rms(o, hn_ref[...]) * _silu(col(3))
            o_ref[0, rows, h * LIN_D:(h + 1) * LIN_D] = o.astype(o_ref.dtype)
        for h in range(LIN_HEADS):
            col = lambda part: p_ref[0, rows, (4 + part) * w + h * LIN_D:(4 + part) * w + (h + 1) * LIN_D].astype(F32)
            o = _ret_chunk(col(0), col(1), col(2), dec_ref[h], qdec_ref[h], kdec_ref[h], cdec_ref[h],
                           st_ref, LIN_HEADS + h)
            o = _rms(o, rn_ref[...]) * _silu(col(3))
            o_ref[0, rows, w + h * LIN_D:w + (h + 1) * LIN_D] = o.astype(o_ref.dtype)
        return carry

    lax.fori_loop(0, tb // LIN_CHUNK, chunk, 0)


def even_mix(proj, lb, hgrn_norm, ret_norm, tb):
    b, t, _ = proj.shape
    c = LIN_CHUNK
    log_gamma = jnp.log(1.0 - jnp.exp2(-5.0 - jnp.arange(LIN_HEADS, dtype=F32)))
    pos = jnp.arange(c, dtype=F32)
    rel = pos[:, None] - pos[None, :]
    dec = jnp.where(rel[None] >= 0, jnp.exp(jnp.maximum(rel, 0.0)[None] * log_gamma[:, None, None]), 0.0)
    ones = jnp.ones((1, 1, LIN_D), F32)
    qdec = jnp.exp((pos + 1.0)[None, :] * log_gamma[:, None])[..., None] * ones
    kdec = jnp.exp((c - 1.0 - pos)[None, :] * log_gamma[:, None])[..., None] * ones
    cdec = jnp.exp(c * log_gamma)[:, None, None] * ones
    const = lambda shape: pl.BlockSpec(shape, lambda i, j: (0,) * len(shape))
    return pl.pallas_call(
        functools.partial(_even_mix_kernel, tb=tb),
        grid=(b, t // tb),
        in_specs=[
            pl.BlockSpec((1, tb, EVEN_IN), lambda i, j: (i, j, 0)),
            const((1, LIN_HEADS * LIN_D)),
            const((1, LIN_D)),
            const((1, LIN_D)),
            const((LIN_HEADS, c, c)),
            const((LIN_HEADS, c, LIN_D)),
            const((LIN_HEADS, c, LIN_D)),
            const((LIN_HEADS, 1, LIN_D)),
        ],
        out_specs=pl.BlockSpec((1, tb, 2 * LIN_HEADS * LIN_D), lambda i, j: (i, j, 0)),
        out_shape=jax.ShapeDtypeStruct((b, t, 2 * LIN_HEADS * LIN_D), BF16),
        scratch_shapes=[pltpu.VMEM((2 * LIN_HEADS, LIN_D, LIN_D), F32)],
        compiler_params=_cparams(("parallel", "arbitrary")),
    )(proj, lb.reshape(1, -1), hgrn_norm.reshape(1, -1), ret_norm.reshape(1, -1), dec, qdec, kdec, cdec)


def _odd_proj_kernel(x_ref, g_ref, wq_ref, wn_ref, wv_ref, wg_ref,
                     q_ref, kc_ref, vc_ref, kk_ref, vv_ref, gl_ref, *, q_scale):
    g_, r_, d = NSA_G, NSA_R, NSA_HD
    xn = _rms(x_ref[0], g_ref[...]).astype(BF16)
    tm = xn.shape[0]
    qt = (lax.dot_general(wq_ref[...], xn, _NT, preferred_element_type=F32) * q_scale).astype(BF16)
    for g in range(g_):
        for r in range(r_):
            rows = slice((g * r_ + r) * d, (g * r_ + r + 1) * d)
            for j in range(tm // QB):
                q_ref[0, g, j, :, r * QB:(r + 1) * QB] = qt[rows, j * QB:(j + 1) * QB]
    nat = jnp.dot(xn, wn_ref[...], preferred_element_type=F32)
    col = lambda i, g: nat[:, (i * g_ + g) * d:(i * g_ + g + 1) * d]
    for g in range(g_):
        kc_ref[0, g] = col(0, g)
        vc_ref[0, g] = col(1, g)
        kk_ref[0, g, SELECTED] = col(2, g).astype(BF16)
        kk_ref[0, g, WINDOWED] = col(3, g).astype(BF16)
    vt = lax.dot_general(wv_ref[...], xn, _NT, preferred_element_type=F32).astype(BF16)
    ones = jnp.ones((VROWS - d, KT), BF16)
    for i in (SELECTED, WINDOWED):
        for g in range(g_):
            for j in range(tm // KT):
                vv_ref[0, g, i, j, :d, :] = vt[(i * g_ + g) * d:(i * g_ + g + 1) * d, j * KT:(j + 1) * KT]
                vv_ref[0, g, i, j, d:, :] = ones
    glt = lax.dot_general(wg_ref[...], xn, _NT, preferred_element_type=F32)
    for g in range(g_):
        for br in range(3):
            for r in range(r_):
                row = (g * 3 + br) * r_ + r
                for j in range(tm // QB):
                    gl_ref[0, g, j, br:br + 1, r * QB:(r + 1) * QB] = glt[row:row + 1, j * QB:(j + 1) * QB]


def odd_project(h, norm_g, w_in, tm, q_scale):
    b, t, dm = h.shape
    g_, r_, d = NSA_G, NSA_R, NSA_HD
    kvw = g_ * d
    o0 = NSA_HEADS * d
    part = lambda i: w_in[:, o0 + i * kvw:o0 + (i + 1) * kvw]
    wq = w_in[:, :o0].T.astype(BF16)
    wn = jnp.concatenate([part(0), part(1), part(2), part(4)], axis=1).astype(BF16)
    wv = jnp.concatenate([part(3), part(5)], axis=1).T.astype(BF16)
    wg = w_in[:, o0 + 6 * kvw:o0 + 6 * kvw + 3 * NSA_HEADS]
    wg = wg.reshape(dm, g_, r_, 3).transpose(1, 3, 2, 0).reshape(g_ * 3 * r_, dm).astype(BF16)
    const = lambda a: pl.BlockSpec(a.shape, lambda i, j: (0,) * a.ndim)
    nqt, nkt = tm // QB, tm // KT
    tok = pl.BlockSpec((1, g_, tm, d), lambda i, j: (i, 0, j, 0))
    tok_shape = jax.ShapeDtypeStruct((b, g_, t, d), F32)
    kk_spec = pl.BlockSpec((1, g_, 2, tm, d), lambda i, j: (i, 0, 0, j, 0))
    kk_shape = jax.ShapeDtypeStruct((b, g_, 2, t, d), BF16)
    vv_spec = pl.BlockSpec((1, g_, 2, nkt, VROWS, KT), lambda i, j: (i, 0, 0, j, 0, 0))
    vv_shape = jax.ShapeDtypeStruct((b, g_, 2, t // KT, VROWS, KT), BF16)
    return pl.pallas_call(
        functools.partial(_odd_proj_kernel, q_scale=q_scale),
        grid=(b, t // tm),
        in_specs=[pl.BlockSpec((1, tm, dm), lambda i, j: (i, j, 0)), pl.BlockSpec((1, dm), lambda i, j: (0, 0)),
                  const(wq), const(wn), const(wv), const(wg)],
        out_specs=[pl.BlockSpec((1, g_, nqt, d, HQ), lambda i, j: (i, 0, j, 0, 0)),
                   tok, tok, kk_spec, vv_spec,
                   pl.BlockSpec((1, g_, nqt, 3, HQ), lambda i, j: (i, 0, j, 0, 0))],
        out_shape=[jax.ShapeDtypeStruct((b, g_, t // QB, d, HQ), BF16),
                   tok_shape, tok_shape, kk_shape, vv_shape,
                   jax.ShapeDtypeStruct((b, g_, t // QB, 3, HQ), F32)],
        compiler_params=_cparams(("parallel", "parallel")),
    )(h, norm_g.reshape(1, dm), wq, wn, wv, wg)


def _compress_kernel(ck_ref, cv_ref, pk_ref, pv_ref, w1k_ref, w1v_ref, w2k_ref, w2v_ref, ok_ref, ov_ref):
    half = CMP_STRIDE * NSA_HD

    def one(c_ref, p_ref, w1_ref, w2_ref):
        ch = c_ref[0, 0]
        a = jnp.dot((ch + p_ref[:, :half]).astype(BF16), w1_ref[:half, :], preferred_element_type=F32)
        b = jnp.dot((ch + p_ref[:, half:]).astype(BF16), w1_ref[half:, :], preferred_element_type=F32)
        nch = a.shape[0]
        pre = a + pltpu.roll(b, nch - 1, axis=0)
        return jnp.dot(_silu(pre).astype(BF16), w2_ref[...], preferred_element_type=F32)

    ok_ref[0, 0] = one(ck_ref, pk_ref, w1k_ref, w2k_ref).astype(ok_ref.dtype)
    ov_ref[0, 0] = one(cv_ref, pv_ref, w1v_ref, w2v_ref).astype(ov_ref.dtype)


def compress(ck, cv, pos_k, pos_v, w1k, w1v, w2k, w2v):
    b, g, nch, width = ck.shape
    blk = pl.BlockSpec((1, 1, nch, width), lambda i, j: (i, j, 0, 0))
    const = lambda shape: pl.BlockSpec(shape, lambda i, j: (0,) * len(shape))
    oblk = pl.BlockSpec((1, 1, nch, NSA_HD), lambda i, j: (i, j, 0, 0))
    oshape = jax.ShapeDtypeStruct((b, g, nch, NSA_HD), BF16)
    return pl.pallas_call(
        _compress_kernel,
        grid=(b, g),
        in_specs=[blk, blk, const((1, 2 * width)), const((1, 2 * width)),
                  const((2 * width, NSA_HD)), const((2 * width, NSA_HD)),
                  const((NSA_HD, NSA_HD)), const((NSA_HD, NSA_HD))],
        out_specs=[oblk, oblk],
        out_shape=[oshape, oshape],
        compiler_params=_cparams(("parallel", "parallel")),
    )(ck, cv, pos_k.reshape(1, -1), pos_v.reshape(1, -1), w1k, w1v, w2k, w2v)


def _cmp_attn_kernel(q_ref, kc_ref, vc_ref, bias_ref, ovt_ref, oc_ref, sel_ref, flag_ref, pslc_ref, *, nq):
    nc = kc_ref.shape[2]
    ns = ovt_ref.shape[0]
    tps = q_ref.shape[2]
    lax.fori_loop(0, tps // 2, functools.partial(_cmp_attn_pair, q_ref, kc_ref, vc_ref, bias_ref, ovt_ref, oc_ref,
                                                 sel_ref, flag_ref, pslc_ref, nq, nc, ns, tps), 0)


def _cmp_attn_pair(q_ref, kc_ref, vc_ref, bias_ref, ovt_ref, oc_ref, sel_ref, flag_ref, pslc_ref,
                   nq, nc, ns, tps, pi, carry):
    n0 = pl.program_id(2) * tps + 2 * pi
    for u in (0, 1):
        _cmp_attend(q_ref, kc_ref, vc_ref, bias_ref, ovt_ref, oc_ref, pslc_ref, nq, nc, 2 * pi + u, n0 + u, u)
    pslc = pslc_ref[...]

    jj = lax.broadcasted_iota(jnp.int32, (2, ns, QB), 1)
    tile = lax.broadcasted_iota(jnp.int32, (2, ns, QB), 0)
    qblk = ((n0 + tile) * QB + lax.broadcasted_iota(jnp.int32, (2, ns, QB), 2)) >> 6
    forced = (jj == 0) | (jj == qblk) | (jj == qblk - 1)
    score = jnp.where(forced, -jnp.inf, jnp.where(jj <= qblk, pslc, NEG_INF))
    sel = jnp.where(forced, 1.0, 0.0)

    def pick_one(score, sel):
        mx = jnp.max(score, axis=1, keepdims=True)
        idx = jnp.min(jnp.where(score == mx, jj, ns), axis=1, keepdims=True)
        pick = jj == idx
        return jnp.where(pick, -jnp.inf, score), jnp.where(pick, 1.0, sel)

    for _ in range(max(min(SLC_TOPN, ns) - 3, 0)):
        score, sel = pick_one(score, sel)
    score, sel = lax.fori_loop(0, jnp.where(n0 == 0, 2, 0), lambda _, c: pick_one(*c), (score, sel))
    for u in (0, 1):
        sel_ref[0, 0, 2 * pi + u] = sel[u]
        flag_ref[0, 0, 2 * pi + u] = jnp.max(sel[u], axis=1, keepdims=True).astype(jnp.int32)
    return carry


def _cmp_attend(q_ref, kc_ref, vc_ref, bias_ref, ovt_ref, oc_ref, pslc_ref, nq, nc, ti, n, slot):
    qt = q_ref[0, 0, ti]
    row0 = pl.multiple_of((QB // CMP_STRIDE) * (nq - 1 - n), QB // CMP_STRIDE)
    tiles_needed = ((QB // CMP_STRIDE) * n + (QB - CMP_BLOCK) // CMP_STRIDE) // CT + 1

    def attend(rows):
        s = jnp.dot(kc_ref[0, 0, :rows, :], qt, preferred_element_type=F32) - bias_ref[0, pl.ds(row0, rows), :]
        m = jnp.maximum(jnp.max(s, axis=0, keepdims=True), 0.1 * NEG_INF)
        e = jnp.exp2(s - m)
        l = jnp.sum(e, axis=0, keepdims=True)
        p = e * (1.0 / jnp.where(l > 0, l, 1.0))
        oc_ref[0, 0, ti] = jnp.dot(vc_ref[0, 0, :, :rows], p.astype(BF16),
                                   preferred_element_type=F32).astype(oc_ref.dtype)
        psum = p[:, 0:QB]
        for r in range(1, NSA_R):
            psum = psum + p[:, r * QB:(r + 1) * QB]
        hi = psum.astype(BF16)
        lo = (psum - hi.astype(F32)).astype(BF16)
        ovt = ovt_ref[:, :rows]
        pslc_ref[slot] = (jnp.dot(ovt, hi, preferred_element_type=F32)
                          + jnp.dot(ovt, lo, preferred_element_type=F32))

    for v in range(1, nc // CT + 1):
        pl.when(tiles_needed == v)(functools.partial(attend, v * CT))


def cmp_attention(q_t, k_cmp, v_cmp_t, cmp_bias, ovt):
    b, g, nq, d, _ = q_t.shape
    nc = k_cmp.shape[2]
    ns = ovt.shape[0]
    assert nc % CT == 0 and cmp_bias.shape[1] == (QB // CMP_STRIDE) * (nq - 1) + nc
    assert nq % TILES_PER_STEP == 0 and TILES_PER_STEP % 2 == 0
    tile = lambda shape: pl.BlockSpec((1, 1, TILES_PER_STEP) + shape, lambda i, j, n: (i, j, n, 0, 0))
    return pl.pallas_call(
        functools.partial(_cmp_attn_kernel, nq=nq),
        grid=(b, g, nq // TILES_PER_STEP),
        in_specs=[
            tile((d, HQ)),
            pl.BlockSpec((1, 1, nc, d), lambda i, j, n: (i, j, 0, 0)),
            pl.BlockSpec((1, 1, d, nc), lambda i, j, n: (i, j, 0, 0)),
            pl.BlockSpec((1,) + cmp_bias.shape[1:], lambda i, j, n: (j, 0, 0)),
            pl.BlockSpec((ns, nc), lambda i, j, n: (0, 0)),
        ],
        out_specs=[tile((d, HQ)), tile((ns, QB)), tile((ns, 1))],
        out_shape=[
            jax.ShapeDtypeStruct((b, g, nq, d, HQ), BF16),
            jax.ShapeDtypeStruct((b, g, nq, ns, QB), F32),
            jax.ShapeDtypeStruct((b, g, nq, ns, 1), jnp.int32),
        ],
        scratch_shapes=[pltpu.VMEM((2, ns, QB), F32)],
        compiler_params=_cparams(("parallel", "parallel", "arbitrary")),
    )(q_t, k_cmp, v_cmp_t, cmp_bias, ovt)


def _scores(qt, kb, bias):
    return jnp.dot(kb, qt, preferred_element_type=F32) - bias


def _flash_step(scores, tiles, state):
    half = SLC_BLOCK
    m_old, acc_old = state
    m_new = m_old
    for s, (_, off, chosen) in zip(scores, tiles):
        mx = jnp.maximum(jnp.where(chosen[0] > 0, jnp.max(s[:half], axis=0, keepdims=True), NEG_INF),
                         jnp.where(chosen[1] > 0, jnp.max(s[half:], axis=0, keepdims=True), NEG_INF))
        m_new = jnp.maximum(m_new, mx - off)
    acc = jnp.exp2(m_old - m_new) * acc_old
    for s, (vtb, off, chosen) in zip(scores, tiles):
        c = m_new + off
        p0 = jnp.exp2(s[:half] - jnp.where(chosen[0] > 0, c, -NEG_INF))
        p1 = jnp.exp2(s[half:] - jnp.where(chosen[1] > 0, c, -NEG_INF))
        acc = acc + jnp.dot(vtb, jnp.concatenate([p0, p1], axis=0).astype(BF16), preferred_element_type=F32)
    return m_new, acc


BIAS_MID, BIAS_DIAG, BIAS_FIRST = 0, 1, 2


SEL_GROUP = 2


def _sel_win_kernel(lists_ref, nstage_ref, q_ref, kk_ref, vv_ref, sel_ref, oc_ref, gl_ref, slope_ref,
                    bias_ref, o_ref, m_ref, acc_ref, sa_ref, sb_ref, *, nq, stride):
    tps = q_ref.shape[2]
    lax.fori_loop(0, tps, functools.partial(_sel_win_tile, lists_ref, nstage_ref, q_ref, kk_ref, vv_ref, sel_ref,
                                            oc_ref, gl_ref, slope_ref, bias_ref, o_ref, m_ref, acc_ref,
                                            sa_ref, sb_ref, nq, stride, tps), 0)


def _sel_win_tile(lists_ref, nstage_ref, q_ref, kk_ref, vv_ref, sel_ref, oc_ref, gl_ref, slope_ref,
                  bias_ref, o_ref, m_ref, acc_ref, sa_ref, sb_ref, nq, stride, tps, ti, carry):
    bi, gi, n = pl.program_id(0), pl.program_id(1), pl.program_id(2) * tps + ti
    qt = q_ref[0, 0, ti]
    slope = slope_ref[0]
    tile_id = (bi * NSA_G + gi) * nq + n
    base = tile_id * stride
    nstage = nstage_ref[tile_id]
    m_ref[...] = jnp.full(m_ref.shape, NEG_INF, F32)
    acc_ref[...] = jnp.zeros(acc_ref.shape, F32)

    def entry(it, u):
        code = lists_ref[base + jnp.minimum(it, nstage) * SEL_GROUP + u]
        c = jnp.maximum(code, 0)
        return c >> CODE_SHIFT, c & ((1 << CODE_SHIFT) - 1), code >= 0

    def score_stage(it, s_ref):
        for u in range(SEL_GROUP):
            br, p, _ = entry(it, u)
            first = (br == WINDOWED) & (p == n - WINDOW // KT)
            kind = jnp.where(p == n, BIAS_DIAG, jnp.where(first, BIAS_FIRST, BIAS_MID))
            rows = pl.ds(pl.multiple_of(p * KT, KT), KT)
            s_ref[u] = _scores(qt, kk_ref[0, 0, br, rows, :], bias_ref[0, kind])

    def softmax_stage(it, s_ref):
        tiles = []
        for u in range(SEL_GROUP):
            br, p, live = entry(it, u)
            keep = jnp.where(live, 1.0, 0.0)
            everyone = jnp.where(br == WINDOWED, 1.0, 0.0)
            chosen = [keep * jnp.maximum(jnp.concatenate([sel_ref[0, 0, ti, pl.ds(2 * p + i, 1), :]] * NSA_R,
                                                         axis=1), everyone) for i in (0, 1)]
            tiles.append((vv_ref[0, 0, br, p], slope * ((n - p) * KT).astype(F32), chosen))
        sb, _, _ = entry(it, 0)
        m, acc = _flash_step([s_ref[u] for u in range(SEL_GROUP)], tiles, (m_ref[sb], acc_ref[sb]))
        m_ref[sb] = m
        acc_ref[sb] = acc

    def body(k, carry):
        score_stage(2 * k + 1, sb_ref)
        softmax_stage(2 * k, sa_ref)
        score_stage(2 * k + 2, sa_ref)
        softmax_stage(2 * k + 1, sb_ref)
        return carry

    score_stage(0, sa_ref)
    lax.fori_loop(0, (nstage + 1) // 2, body, 0)
    d = NSA_HD
    o_s = acc_ref[SELECTED, :d, :] * (1.0 / acc_ref[SELECTED, d:d + 1, :])
    o_w = acc_ref[WINDOWED, :d, :] * (1.0 / acc_ref[WINDOWED, d:d + 1, :])

    gates = jax.nn.sigmoid(gl_ref[0, 0, ti])
    o = gates[0:1] * oc_ref[0, 0, ti].astype(F32) + gates[1:2] * o_s + gates[2:3] * o_w
    pieces = [jnp.concatenate([o[:, r * QB:(r + 1) * QB], o[:, (r + 1) * QB:(r + 2) * QB]], axis=0).T
              for r in range(0, NSA_R, 2)]
    o_ref[0, pl.ds(pl.multiple_of(ti * QB, QB), QB), :] = jnp.concatenate(pieces, axis=1).astype(o_ref.dtype)
    return carry


def tile_lists(pair_flags):
    nq = pair_flags.shape[-1]
    nwin = WINDOW // KT
    n = jnp.arange(nq, dtype=jnp.int32)[:, None]
    wslots = -(-(nwin + 1) // SEL_GROUP) * SEL_GROUP
    wp = n - jnp.arange(wslots, dtype=jnp.int32)[None, :]
    wcode = jnp.where((wp >= 0) & (wp >= n - nwin), (WINDOWED << CODE_SHIFT) + wp, -1)
    p = jnp.arange(nq, dtype=jnp.int32)[None, :]
    active = ((pair_flags != 0) & (p < n)) | (p == n)
    order = jnp.sort(jnp.where(active, p, nq), axis=-1)
    sslots = -(-nq // SEL_GROUP) * SEL_GROUP
    order = jnp.pad(order, ((0, 0),) * 3 + ((0, sslots - nq),), constant_values=nq)
    scode = jnp.where(order < nq, (SELECTED << CODE_SHIFT) + order, -1)
    lead = scode.shape[:3]
    codes = jnp.concatenate([jnp.broadcast_to(wcode, lead + (wslots,)), scode,
                             jnp.full(lead + (2 * SEL_GROUP,), -1, jnp.int32)], axis=-1).astype(jnp.int32)
    cnt = jnp.sum(active, axis=-1).astype(jnp.int32)
    stages = wslots // SEL_GROUP + (cnt + SEL_GROUP - 1) // SEL_GROUP
    return codes.reshape(-1), stages.reshape(-1), codes.shape[-1]


def sel_win_attention(codes, stages, stride, q_t, kk, vv, sel, o_c, gl, slope_row, bias):
    b, g, nq, d, _ = q_t.shape
    t = kk.shape[3]
    ns = sel.shape[3]
    assert nq < (1 << CODE_SHIFT)
    tps = TILES_PER_STEP
    tile = lambda shape: pl.BlockSpec((1, 1, tps) + shape, lambda i, j, n, c, s: (i, j, n, 0, 0))
    grid_spec = pltpu.PrefetchScalarGridSpec(
        num_scalar_prefetch=2,
        grid=(b, g, nq // tps),
        in_specs=[
            tile((d, HQ)),
            pl.BlockSpec((1, 1, 2, t, d), lambda i, j, n, c, s: (i, j, 0, 0, 0)),
            pl.BlockSpec((1, 1, 2, t // KT, VROWS, KT), lambda i, j, n, c, s: (i, j, 0, 0, 0, 0)),
            tile((ns, QB)), tile((d, HQ)), tile((3, HQ)),
            pl.BlockSpec((1, 1, HQ), lambda i, j, n, c, s: (j, 0, 0)),
            pl.BlockSpec((1, 3, KT, HQ), lambda i, j, n, c, s: (j, 0, 0, 0)),
        ],
        out_specs=pl.BlockSpec((1, tps * QB, NSA_R * d), lambda i, j, n, c, s: (i, n, j)),
        scratch_shapes=[pltpu.VMEM((2, 1, HQ), F32), pltpu.VMEM((2, VROWS, HQ), F32),
                        pltpu.VMEM((SEL_GROUP, KT, HQ), F32), pltpu.VMEM((SEL_GROUP, KT, HQ), F32)],
    )
    return pl.pallas_call(
        functools.partial(_sel_win_kernel, nq=nq, stride=stride),
        grid_spec=grid_spec,
        out_shape=jax.ShapeDtypeStruct((b, t, g * NSA_R * d), BF16),
        compiler_params=_cparams(("parallel", "parallel", "arbitrary")),
    )(codes, stages, q_t, kk, vv, sel, o_c, gl, slope_row, bias)


def _row_tile(m):
    return 512 if m % 512 == 0 else m


def even_layer_heads(h, norm_g, w_in, lb, hgrn_norm, ret_norm):
    b, t, dm = h.shape
    proj = norm_matmul(h.reshape(b * t, dm), norm_g, w_in.astype(BF16), _row_tile(b * t), 1024, BF16)
    o = even_mix(proj.reshape(b, t, EVEN_IN), lb, hgrn_norm, ret_norm, min(t, 512))
    return o.reshape(b * t, -1)


def odd_layer_heads(h, norm_g, w_in, cmp_pos_k, cmp_pos_v, cmp_w1_k, cmp_w2_k, cmp_w1_v, cmp_w2_v):
    b, t, dm = h.shape
    g_, r_, d = NSA_G, NSA_R, NSA_HD
    nq = t // QB
    ns = t // SLC_BLOCK
    nch = t // CMP_STRIDE
    q_t, kc, vc, kk, vv, gl_t = odd_project(h, norm_g, w_in, min(t, 512), (d ** -0.5) * LOG2E)
    chunks = lambda a: a.reshape(b, g_, nch, CMP_STRIDE * d)
    k_cmp, v_cmp = compress(chunks(kc), chunks(vc), cmp_pos_k, cmp_pos_v,
                            cmp_w1_k.astype(BF16), cmp_w1_v.astype(BF16),
                            cmp_w2_k.astype(BF16), cmp_w2_v.astype(BF16))
    v_cmp_t = v_cmp.transpose(0, 1, 3, 2)

    slopes = jnp.exp2(-8.0 * jnp.arange(1, NSA_HEADS + 1, dtype=F32) / NSA_HEADS).reshape(g_, r_)
    slope_row = jnp.repeat(slopes, QB, axis=1).reshape(g_, 1, HQ) * LOG2E
    tq = np.tile(np.arange(QB), r_)[None, :]
    c_rel = np.arange(-(QB // CMP_STRIDE) * (nq - 1), nch)[:, None]
    dist_rel = tq - (c_rel * CMP_STRIDE + CMP_BLOCK - 1)
    cmp_bias = (slope_row * jnp.asarray(dist_rel, F32)[None]
                + jnp.asarray(np.where(dist_rel >= 0, 0.0, -NEG_INF), F32)[None])
    rel = tq - np.arange(KT)[:, None]
    mid = slope_row * jnp.asarray(rel, F32)[None]
    bias = jnp.stack([mid, mid + jnp.asarray(np.where(rel >= 0, 0.0, -NEG_INF), F32),
                      mid + jnp.asarray(np.where(rel < 0, 0.0, -NEG_INF), F32)], axis=1)
    c0 = np.arange(nch) * CMP_STRIDE
    s0 = np.arange(ns) * SLC_BLOCK
    ov = np.maximum(np.minimum(c0[:, None] + CMP_BLOCK, s0[None, :] + SLC_BLOCK)
                    - np.maximum(c0[:, None], s0[None, :]), 0) / CMP_BLOCK
    ov[nch - (CMP_BLOCK // CMP_STRIDE - 1):] = 0.0
    ovt = jnp.asarray(ov.T, BF16)

    o_c, sel, flags = cmp_attention(q_t, k_cmp, v_cmp_t, cmp_bias, ovt)
    codes, stages, stride = tile_lists(flags.reshape(b, g_, nq, ns // 2, 2).max(axis=-1))
    o = sel_win_attention(codes, stages, stride, q_t, kk, vv, sel, o_c, gl_t, slope_row, bias)
    return o.reshape(b * t, NSA_HEADS * d)


def finish_layer(h, heads, w_out, norm_g, w_gate_up, w_down, final_g):
    b, t, dm = h.shape
    m = b * t
    out = mixer_out_ffn(h.reshape(m, dm), heads, w_out.astype(BF16), norm_g, w_gate_up.astype(BF16),
                        w_down.astype(BF16), final_g, 1024 if m % 1024 == 0 else m, 256)
    return out.reshape(b, t, dm)


def kernel(x, mix_norm, ffn_norm, final_norm, even_w_in, hgrn_lower_bounds, hgrn_out_norm, ret_out_norm,
           even_w_out, odd_w_in, cmp_pos_k, cmp_pos_v, cmp_w1_k, cmp_w2_k, cmp_w1_v, cmp_w2_v, odd_w_out,
           ffn_w_gate_up, ffn_w_down):
    depth = mix_norm.shape[0]
    lb_all = jnp.cumsum(jax.nn.softmax(hgrn_lower_bounds.astype(F32), axis=0), axis=0)
    h = x
    for layer in range(depth):
        if layer % 2 == 0:
            e = layer // 2
            heads = even_layer_heads(h, mix_norm[layer], even_w_in[e], lb_all[e], hgrn_out_norm[e], ret_out_norm[e])
            w_out = even_w_out[e]
        else:
            o = layer // 2
            heads = odd_layer_heads(h, mix_norm[layer], odd_w_in[o], cmp_pos_k[o], cmp_pos_v[o], cmp_w1_k[o],
                                    cmp_w2_k[o], cmp_w1_v[o], cmp_w2_v[o])
            w_out = odd_w_out[o]
        h = finish_layer(h, heads, w_out, ffn_norm[layer], ffn_w_gate_up[layer], ffn_w_down[layer],
                         final_norm if layer == depth - 1 else None)
    return h
```

```python
import functools

import numpy as np
import jax
import jax.numpy as jnp
from jax import lax
from jax.experimental import pallas as pl
from jax.experimental.pallas import tpu as pltpu

F32 = jnp.float32
BF16 = jnp.bfloat16

D_MODEL = 1024
RMS_EPS = 1e-6
NEG_INF = -1e30
FORCE_SCORE = 1e9

LIN_HEADS = 4
LIN_D = 128
LIN_CHUNK = 64
SUB = 16
EVEN_IN = 8 * LIN_HEADS * LIN_D

NSA_HD = 64
NSA_HEADS = 16
NSA_G = 2
NSA_R = NSA_HEADS // NSA_G
CMP_BLOCK = 32
CMP_STRIDE = 16
SLC_BLOCK = 64
SLC_TOPN = 16
WINDOW = 512
QB = 128
KT = 128
CT = 64
TILES_PER_STEP = 8
SELECTED, WINDOWED = 0, 1
VROWS = NSA_HD + 16
CODE_SHIFT = 10
HQ = NSA_R * QB
LOG2E = 1.4426950408889634

FFN_HIDDEN = 2816

VMEM_LIMIT = 56 * 1024 * 1024

_NT = (((1,), (1,)), ((), ()))
_TN = (((0,), (0,)), ((), ()))


def _cparams(sem):
    return pltpu.CompilerParams(dimension_semantics=sem, vmem_limit_bytes=VMEM_LIMIT)


def _rms(x, g):
    ms = jnp.mean(x * x, axis=-1, keepdims=True)
    return x * lax.rsqrt(ms + RMS_EPS) * g


def _silu(x):
    return x * jax.nn.sigmoid(x)


def _norm_matmul_kernel(x_ref, g_ref, w_ref, o_ref, *, tn):
    xn = _rms(x_ref[...], g_ref[...]).astype(BF16)
    for c in range(w_ref.shape[1] // tn):
        cols = slice(c * tn, (c + 1) * tn)
        o_ref[:, cols] = jnp.dot(xn, w_ref[:, cols], preferred_element_type=F32).astype(o_ref.dtype)


def norm_matmul(x, g, w, tm, tn, out_dtype):
    m, k = x.shape
    n = w.shape[1]
    return pl.pallas_call(
        functools.partial(_norm_matmul_kernel, tn=tn),
        grid=(m // tm,),
        in_specs=[
            pl.BlockSpec((tm, k), lambda i: (i, 0)),
            pl.BlockSpec((1, k), lambda i: (0, 0)),
            pl.BlockSpec((k, n), lambda i: (0, 0)),
        ],
        out_specs=pl.BlockSpec((tm, n), lambda i: (i, 0)),
        out_shape=jax.ShapeDtypeStruct((m, n), out_dtype),
        compiler_params=_cparams(("parallel",)),
    )(x, g.reshape(1, k), w)


def _ffn_kernel(x_ref, a_ref, wo_ref, g_ref, wgu_ref, wd_ref, fg_ref, o_ref, xn_ref, acc_ref, *, final_norm):
    j = pl.program_id(1)
    th = wd_ref.shape[0]

    @pl.when(j == 0)
    def _():
        h = x_ref[...] + jnp.dot(a_ref[...], wo_ref[...], preferred_element_type=F32)
        xn_ref[...] = _rms(h, g_ref[...]).astype(BF16)
        acc_ref[...] = h

    xn = xn_ref[...]
    gate = jnp.dot(xn, wgu_ref[0, :, :th], preferred_element_type=F32)
    up = jnp.dot(xn, wgu_ref[0, :, th:], preferred_element_type=F32)
    a = (_silu(gate) * up).astype(BF16)
    acc_ref[...] += jnp.dot(a, wd_ref[...], preferred_element_type=F32)

    @pl.when(j == pl.num_programs(1) - 1)
    def _():
        h = acc_ref[...]
        if final_norm:
            h = _rms(h, fg_ref[...])
        o_ref[...] = h


def mixer_out_ffn(x, a, w_out, g, w_gate_up, w_down, final_g, tm, th):
    m, k = x.shape
    hid = w_down.shape[0]
    nj = hid // th
    final_norm = final_g is not None
    fg = final_g if final_norm else g
    w_gu = w_gate_up.reshape(k, 2, nj, th).transpose(2, 0, 1, 3).reshape(nj, k, 2 * th)
    return pl.pallas_call(
        functools.partial(_ffn_kernel, final_norm=final_norm),
        grid=(m // tm, nj),
        in_specs=[
            pl.BlockSpec((tm, k), lambda i, j: (i, 0)),
            pl.BlockSpec((tm, k), lambda i, j: (i, 0)),
            pl.BlockSpec((k, k), lambda i, j: (0, 0)),
            pl.BlockSpec((1, k), lambda i, j: (0, 0)),
            pl.BlockSpec((1, k, 2 * th), lambda i, j: (j, 0, 0)),
            pl.BlockSpec((th, k), lambda i, j: (j, 0)),
            pl.BlockSpec((1, k), lambda i, j: (0, 0)),
        ],
        out_specs=pl.BlockSpec((tm, k), lambda i, j: (i, 0)),
        out_shape=jax.ShapeDtypeStruct((m, k), F32),
        scratch_shapes=[pltpu.VMEM((tm, k), BF16), pltpu.VMEM((tm, k), F32)],
        compiler_params=_cparams(("parallel", "arbitrary")),
    )(x, a, w_out, g.reshape(1, k), w_gu, w_down, fg.reshape(1, k))


def _cumsum_rows(g):
    c = g.shape[0]
    row = lax.broadcasted_iota(jnp.int32, g.shape, 0)
    s = 1
    while s < c:
        g = g + jnp.where(row >= s, pltpu.roll(g, s, axis=0), 0.0)
        s *= 2
    return g


def _hgrn_chunk(hq, hf, hi, lb, st_ref, h):
    c = LIN_CHUNK
    f = lb + (1.0 - lb) * jax.nn.sigmoid(hf)
    k = 1.0 - f
    q = _silu(hq)
    v = hi
    cum = _cumsum_rows(jnp.log2(f))
    st = st_ref[h]
    o_inter = lax.dot_general((q * jnp.exp2(cum)).astype(BF16), st.astype(BF16), _NT,
                              preferred_element_type=F32)
    irow = lax.broadcasted_iota(jnp.int32, (SUB, LIN_D), 0)
    krow = lax.broadcasted_iota(jnp.int32, (c, LIN_D), 0)
    lane = lax.broadcasted_iota(jnp.int32, (SUB, c), 1)
    vb = v.astype(BF16)
    attn_rows = []
    for a in range(c // SUB):
        lo = a * SUB
        cum_a = cum[lo:lo + SUB]
        q_a = q[lo:lo + SUB]
        k_a = k[lo:lo + SUB]
        if a > 0:
            ref = cum[lo - 1:lo, :]
            qd = (q_a * jnp.exp2(cum_a - ref)).astype(BF16)
            kd = (k * jnp.exp2(jnp.where(krow < lo, ref - cum, NEG_INF))).astype(BF16)
            attn = lax.dot_general(qd, kd, _NT, preferred_element_type=F32)
        else:
            attn = jnp.zeros((SUB, c), F32)
        for j in range(SUB):
            dec = jnp.exp2(jnp.where(irow >= j, cum_a - cum_a[j:j + 1, :], NEG_INF))
            w = (q_a * k_a[j:j + 1, :]) * dec
            attn = jnp.where(lane == lo + j, jnp.sum(w, axis=-1, keepdims=True), attn)
        attn_rows.append(attn)
    attn = jnp.concatenate(attn_rows, axis=0).astype(BF16)
    o_intra = jnp.dot(attn, vb, preferred_element_type=F32)
    last = cum[c - 1:c, :]
    kd = (k * jnp.exp2(last - cum)).astype(BF16)
    st_ref[h] = st * jnp.exp2(last) + lax.dot_general(vb, kd, _TN, preferred_element_type=F32)
    return o_intra + o_inter


def _ret_chunk(rq, rk, rv, dec, qdec, kdec, cdec, st_ref, h):
    k = rk * (LIN_D ** -0.5)
    st = st_ref[h]
    qb = rq.astype(BF16)
    vb = rv.astype(BF16)
    attn = lax.dot_general(qb, k.astype(BF16), _NT, preferred_element_type=F32) * dec
    o = jnp.dot(attn.astype(BF16), vb, preferred_element_type=F32)
    o = o + lax.dot_general((rq * qdec).astype(BF16), st.astype(BF16), _NT, preferred_element_type=F32)
    st_ref[h] = st * cdec + lax.dot_general(vb, (k * kdec).astype(BF16), _TN, preferred_element_type=F32)
    return o


def _even_mix_kernel(p_ref, lb_ref, hn_ref, rn_ref, dec_ref, qdec_ref, kdec_ref, cdec_ref,
                     o_ref, st_ref, *, tb):
    @pl.when(pl.program_id(1) == 0)
    def _():
        st_ref[...] = jnp.zeros_like(st_ref)

    w = LIN_HEADS * LIN_D

    def chunk(ci, carry):
        rows = pl.ds(pl.multiple_of(ci * LIN_CHUNK, LIN_CHUNK), LIN_CHUNK)
        for h in range(LIN_HEADS):
            col = lambda part: p_ref[0, rows, part * w + h * LIN_D:part * w + (h + 1) * LIN_D].astype(F32)
            o = _hgrn_chunk(col(0), col(1), col(2), lb_ref[:, h * LIN_D:(h + 1) * LIN_D], st_ref, h)
            o = _rms(o, hn_ref[...]) * _silu(col(3))
            o_ref[0, rows, h * LIN_D:(h + 1) * LIN_D] = o.astype(o_ref.dtype)
        for h in range(LIN_HEADS):
            col = lambda part: p_ref[0, rows, (4 + part) * w + h * LIN_D:(4 + part) * w + (h + 1) * LIN_D].astype(F32)
            o = _ret_chunk(col(0), col(1), col(2), dec_ref[h], qdec_ref[h], kdec_ref[h], cdec_ref[h],
                           st_ref, LIN_HEADS + h)
            o = _rms(o, rn_ref[...]) * _silu(col(3))
            o_ref[0, rows, w + h * LIN_D:w + (h + 1) * LIN_D] = o.astype(o_ref.dtype)
        return carry

    lax.fori_loop(0, tb // LIN_CHUNK, chunk, 0)


def even_mix(proj, lb, hgrn_norm, ret_norm, tb):
    b, t, _ = proj.shape
    c = LIN_CHUNK
    log_gamma = jnp.log(1.0 - jnp.exp2(-5.0 - jnp.arange(LIN_HEADS, dtype=F32)))
    pos = jnp.arange(c, dtype=F32)
    rel = pos[:, None] - pos[None, :]
    dec = jnp.where(rel[None] >= 0, jnp.exp(jnp.maximum(rel, 0.0)[None] * log_gamma[:, None, None]), 0.0)
    ones = jnp.ones((1, 1, LIN_D), F32)
    qdec = jnp.exp((pos + 1.0)[None, :] * log_gamma[:, None])[..., None] * ones
    kdec = jnp.exp((c - 1.0 - pos)[None, :] * log_gamma[:, None])[..., None] * ones
    cdec = jnp.exp(c * log_gamma)[:, None, None] * ones
    const = lambda shape: pl.BlockSpec(shape, lambda i, j: (0,) * len(shape))
    return pl.pallas_call(
        functools.partial(_even_mix_kernel, tb=tb),
        grid=(b, t // tb),
        in_specs=[
            pl.BlockSpec((1, tb, EVEN_IN), lambda i, j: (i, j, 0)),
            const((1, LIN_HEADS * LIN_D)),
            const((1, LIN_D)),
            const((1, LIN_D)),
            const((LIN_HEADS, c, c)),
            const((LIN_HEADS, c, LIN_D)),
            const((LIN_HEADS, c, LIN_D)),
            const((LIN_HEADS, 1, LIN_D)),
        ],
        out_specs=pl.BlockSpec((1, tb, 2 * LIN_HEADS * LIN_D), lambda i, j: (i, j, 0)),
        out_shape=jax.ShapeDtypeStruct((b, t, 2 * LIN_HEADS * LIN_D), BF16),
        scratch_shapes=[pltpu.VMEM((2 * LIN_HEADS, LIN_D, LIN_D), F32)],
        compiler_params=_cparams(("parallel", "arbitrary")),
    )(proj, lb.reshape(1, -1), hgrn_norm.reshape(1, -1), ret_norm.reshape(1, -1), dec, qdec, kdec, cdec)


def _odd_proj_kernel(x_ref, g_ref, wq_ref, wn_ref, wv_ref, wg_ref,
                     q_ref, kc_ref, vc_ref, kk_ref, vv_ref, gl_ref, *, q_scale):
    g_, r_, d = NSA_G, NSA_R, NSA_HD
    xn = _rms(x_ref[0], g_ref[...]).astype(BF16)
    tm = xn.shape[0]
    qt = (lax.dot_general(wq_ref[...], xn, _NT, preferred_element_type=F32) * q_scale).astype(BF16)
    for g in range(g_):
        for r in range(r_):
            rows = slice((g * r_ + r) * d, (g * r_ + r + 1) * d)
            for j in range(tm // QB):
                q_ref[0, g, j, :, r * QB:(r + 1) * QB] = qt[rows, j * QB:(j + 1) * QB]
    nat = jnp.dot(xn, wn_ref[...], preferred_element_type=F32)
    col = lambda i, g: nat[:, (i * g_ + g) * d:(i * g_ + g + 1) * d]
    for g in range(g_):
        kc_ref[0, g] = col(0, g)
        vc_ref[0, g] = col(1, g)
        kk_ref[0, g, SELECTED] = col(2, g).astype(BF16)
        kk_ref[0, g, WINDOWED] = col(3, g).astype(BF16)
    vt = lax.dot_general(wv_ref[...], xn, _NT, preferred_element_type=F32).astype(BF16)
    ones = jnp.ones((VROWS - d, KT), BF16)
    for i in (SELECTED, WINDOWED):
        for g in range(g_):
            for j in range(tm // KT):
                vv_ref[0, g, i, j, :d, :] = vt[(i * g_ + g) * d:(i * g_ + g + 1) * d, j * KT:(j + 1) * KT]
                vv_ref[0, g, i, j, d:, :] = ones
    glt = lax.dot_general(wg_ref[...], xn, _NT, preferred_element_type=F32)
    for g in range(g_):
        for br in range(3):
            for r in range(r_):
                row = (g * 3 + br) * r_ + r
                for j in range(tm // QB):
                    gl_ref[0, g, j, br:br + 1, r * QB:(r + 1) * QB] = glt[row:row + 1, j * QB:(j + 1) * QB]


def odd_project(h, norm_g, w_in, tm, q_scale):
    b, t, dm = h.shape
    g_, r_, d = NSA_G, NSA_R, NSA_HD
    kvw = g_ * d
    o0 = NSA_HEADS * d
    part = lambda i: w_in[:, o0 + i * kvw:o0 + (i + 1) * kvw]
    wq = w_in[:, :o0].T.astype(BF16)
    wn = jnp.concatenate([part(0), part(1), part(2), part(4)], axis=1).astype(BF16)
    wv = jnp.concatenate([part(3), part(5)], axis=1).T.astype(BF16)
    wg = w_in[:, o0 + 6 * kvw:o0 + 6 * kvw + 3 * NSA_HEADS]
    wg = wg.reshape(dm, g_, r_, 3).transpose(1, 3, 2, 0).reshape(g_ * 3 * r_, dm).astype(BF16)
    const = lambda a: pl.BlockSpec(a.shape, lambda i, j: (0,) * a.ndim)
    nqt, nkt = tm // QB, tm // KT
    tok = pl.BlockSpec((1, g_, tm, d), lambda i, j: (i, 0, j, 0))
    tok_shape = jax.ShapeDtypeStruct((b, g_, t, d), F32)
    kk_spec = pl.BlockSpec((1, g_, 2, tm, d), lambda i, j: (i, 0, 0, j, 0))
    kk_shape = jax.ShapeDtypeStruct((b, g_, 2, t, d), BF16)
    vv_spec = pl.BlockSpec((1, g_, 2, nkt, VROWS, KT), lambda i, j: (i, 0, 0, j, 0, 0))
    vv_shape = jax.ShapeDtypeStruct((b, g_, 2, t // KT, VROWS, KT), BF16)
    return pl.pallas_call(
        functools.partial(_odd_proj_kernel, q_scale=q_scale),
        grid=(b, t // tm),
        in_specs=[pl.BlockSpec((1, tm, dm), lambda i, j: (i, j, 0)), pl.BlockSpec((1, dm), lambda i, j: (0, 0)),
                  const(wq), const(wn), const(wv), const(wg)],
        out_specs=[pl.BlockSpec((1, g_, nqt, d, HQ), lambda i, j: (i, 0, j, 0, 0)),
                   tok, tok, kk_spec, vv_spec,
                   pl.BlockSpec((1, g_, nqt, 3, HQ), lambda i, j: (i, 0, j, 0, 0))],
        out_shape=[jax.ShapeDtypeStruct((b, g_, t // QB, d, HQ), BF16),
                   tok_shape, tok_shape, kk_shape, vv_shape,
                   jax.ShapeDtypeStruct((b, g_, t // QB, 3, HQ), F32)],
        compiler_params=_cparams(("parallel", "parallel")),
    )(h, norm_g.reshape(1, dm), wq, wn, wv, wg)


def _compress_kernel(ck_ref, cv_ref, pk_ref, pv_ref, w1k_ref, w1v_ref, w2k_ref, w2v_ref, ok_ref, ov_ref):
    half = CMP_STRIDE * NSA_HD

    def one(c_ref, p_ref, w1_ref, w2_ref):
        ch = c_ref[0, 0]
        a = jnp.dot((ch + p_ref[:, :half]).astype(BF16), w1_ref[:half, :], preferred_element_type=F32)
        b = jnp.dot((ch + p_ref[:, half:]).astype(BF16), w1_ref[half:, :], preferred_element_type=F32)
        nch = a.shape[0]
        pre = a + pltpu.roll(b, nch - 1, axis=0)
        return jnp.dot(_silu(pre).astype(BF16), w2_ref[...], preferred_element_type=F32)

    ok_ref[0, 0] = one(ck_ref, pk_ref, w1k_ref, w2k_ref).astype(ok_ref.dtype)
    ov_ref[0, 0] = one(cv_ref, pv_ref, w1v_ref, w2v_ref).astype(ov_ref.dtype)


def compress(ck, cv, pos_k, pos_v, w1k, w1v, w2k, w2v):
    b, g, nch, width = ck.shape
    blk = pl.BlockSpec((1, 1, nch, width), lambda i, j: (i, j, 0, 0))
    const = lambda shape: pl.BlockSpec(shape, lambda i, j: (0,) * len(shape))
    oblk = pl.BlockSpec((1, 1, nch, NSA_HD), lambda i, j: (i, j, 0, 0))
    oshape = jax.ShapeDtypeStruct((b, g, nch, NSA_HD), BF16)
    return pl.pallas_call(
        _compress_kernel,
        grid=(b, g),
        in_specs=[blk, blk, const((1, 2 * width)), const((1, 2 * width)),
                  const((2 * width, NSA_HD)), const((2 * width, NSA_HD)),
                  const((NSA_HD, NSA_HD)), const((NSA_HD, NSA_HD))],
        out_specs=[oblk, oblk],
        out_shape=[oshape, oshape],
        compiler_params=_cparams(("parallel", "parallel")),
    )(ck, cv, pos_k.reshape(1, -1), pos_v.reshape(1, -1), w1k, w1v, w2k, w2v)


def _cmp_attn_kernel(q_ref, kc_ref, vc_ref, bias_ref, ovt_ref, oc_ref, sel_ref, flag_ref, pslc_ref, *, nq):
    nc = kc_ref.shape[2]
    ns = ovt_ref.shape[0]
    tps = q_ref.shape[2]
    lax.fori_loop(0, tps // 2, functools.partial(_cmp_attn_pair, q_ref, kc_ref, vc_ref, bias_ref, ovt_ref, oc_ref,
                                                 sel_ref, flag_ref, pslc_ref, nq, nc, ns, tps), 0)


def _cmp_attn_pair(q_ref, kc_ref, vc_ref, bias_ref, ovt_ref, oc_ref, sel_ref, flag_ref, pslc_ref,
                   nq, nc, ns, tps, pi, carry):
    n0 = pl.program_id(2) * tps + 2 * pi
    for u in (0, 1):
        _cmp_attend(q_ref, kc_ref, vc_ref, bias_ref, ovt_ref, oc_ref, pslc_ref, nq, nc, 2 * pi + u, n0 + u, u)
    pslc = pslc_ref[...]

    jj = lax.broadcasted_iota(jnp.int32, (2, ns, QB), 1)
    tile = lax.broadcasted_iota(jnp.int32, (2, ns, QB), 0)
    qblk = ((n0 + tile) * QB + lax.broadcasted_iota(jnp.int32, (2, ns, QB), 2)) >> 6
    forced = (jj == 0) | (jj == qblk) | (jj == qblk - 1)
    score = jnp.where(forced, -jnp.inf, jnp.where(jj <= qblk, pslc, NEG_INF))
    sel = jnp.where(forced, 1.0, 0.0)

    def pick_one(score, sel):
        mx = jnp.max(score, axis=1, keepdims=True)
        idx = jnp.min(jnp.where(score == mx, jj, ns), axis=1, keepdims=True)
        pick = jj == idx
        return jnp.where(pick, -jnp.inf, score), jnp.where(pick, 1.0, sel)

    for _ in range(max(min(SLC_TOPN, ns) - 3, 0)):
        score, sel = pick_one(score, sel)
    score, sel = lax.fori_loop(0, jnp.where(n0 == 0, 2, 0), lambda _, c: pick_one(*c), (score, sel))
    for u in (0, 1):
        sel_ref[0, 0, 2 * pi + u] = sel[u]
        flag_ref[0, 0, 2 * pi + u] = jnp.max(sel[u], axis=1, keepdims=True).astype(jnp.int32)
    return carry


def _cmp_attend(q_ref, kc_ref, vc_ref, bias_ref, ovt_ref, oc_ref, pslc_ref, nq, nc, ti, n, slot):
    qt = q_ref[0, 0, ti]
    row0 = pl.multiple_of((QB // CMP_STRIDE) * (nq - 1 - n), QB // CMP_STRIDE)
    tiles_needed = ((QB // CMP_STRIDE) * n + (QB - CMP_BLOCK) // CMP_STRIDE) // CT + 1

    def attend(rows):
        s = jnp.dot(kc_ref[0, 0, :rows, :], qt, preferred_element_type=F32) - bias_ref[0, pl.ds(row0, rows), :]
        m = jnp.maximum(jnp.max(s, axis=0, keepdims=True), 0.1 * NEG_INF)
        e = jnp.exp2(s - m)
        l = jnp.sum(e, axis=0, keepdims=True)
        p = e * (1.0 / jnp.where(l > 0, l, 1.0))
        oc_ref[0, 0, ti] = jnp.dot(vc_ref[0, 0, :, :rows], p.astype(BF16),
                                   preferred_element_type=F32).astype(oc_ref.dtype)
        psum = p[:, 0:QB]
        for r in range(1, NSA_R):
            psum = psum + p[:, r * QB:(r + 1) * QB]
        hi = psum.astype(BF16)
        lo = (psum - hi.astype(F32)).astype(BF16)
        ovt = ovt_ref[:, :rows]
        pslc_ref[slot] = (jnp.dot(ovt, hi, preferred_element_type=F32)
                          + jnp.dot(ovt, lo, preferred_element_type=F32))

    for v in range(1, nc // CT + 1):
        pl.when(tiles_needed == v)(functools.partial(attend, v * CT))


def cmp_attention(q_t, k_cmp, v_cmp_t, cmp_bias, ovt):
    b, g, nq, d, _ = q_t.shape
    nc = k_cmp.shape[2]
    ns = ovt.shape[0]
    assert nc % CT == 0 and cmp_bias.shape[1] == (QB // CMP_STRIDE) * (nq - 1) + nc
    assert nq % TILES_PER_STEP == 0 and TILES_PER_STEP % 2 == 0
    tile = lambda shape: pl.BlockSpec((1, 1, TILES_PER_STEP) + shape, lambda i, j, n: (i, j, n, 0, 0))
    return pl.pallas_call(
        functools.partial(_cmp_attn_kernel, nq=nq),
        grid=(b, g, nq // TILES_PER_STEP),
        in_specs=[
            tile((d, HQ)),
            pl.BlockSpec((1, 1, nc, d), lambda i, j, n: (i, j, 0, 0)),
            pl.BlockSpec((1, 1, d, nc), lambda i, j, n: (i, j, 0, 0)),
            pl.BlockSpec((1,) + cmp_bias.shape[1:], lambda i, j, n: (j, 0, 0)),
            pl.BlockSpec((ns, nc), lambda i, j, n: (0, 0)),
        ],
        out_specs=[tile((d, HQ)), tile((ns, QB)), tile((ns, 1))],
        out_shape=[
            jax.ShapeDtypeStruct((b, g, nq, d, HQ), BF16),
            jax.ShapeDtypeStruct((b, g, nq, ns, QB), F32),
            jax.ShapeDtypeStruct((b, g, nq, ns, 1), jnp.int32),
        ],
        scratch_shapes=[pltpu.VMEM((2, ns, QB), F32)],
        compiler_params=_cparams(("parallel", "parallel", "arbitrary")),
    )(q_t, k_cmp, v_cmp_t, cmp_bias, ovt)


def _scores(qt, kb, bias):
    return jnp.dot(kb, qt, preferred_element_type=F32) - bias


def _flash_step(scores, tiles, state):
    half = SLC_BLOCK
    m_old, acc_old = state
    m_new = m_old
    for s, (_, off, chosen) in zip(scores, tiles):
        mx = jnp.maximum(jnp.where(chosen[0] > 0, jnp.max(s[:half], axis=0, keepdims=True), NEG_INF),
                         jnp.where(chosen[1] > 0, jnp.max(s[half:], axis=0, keepdims=True), NEG_INF))
        m_new = jnp.maximum(m_new, mx - off)
    acc = jnp.exp2(m_old - m_new) * acc_old
    for s, (vtb, off, chosen) in zip(scores, tiles):
        c = m_new + off
        p0 = jnp.exp2(s[:half] - jnp.where(chosen[0] > 0, c, -NEG_INF))
        p1 = jnp.exp2(s[half:] - jnp.where(chosen[1] > 0, c, -NEG_INF))
        acc = acc + jnp.dot(vtb, jnp.concatenate([p0, p1], axis=0).astype(BF16), preferred_element_type=F32)
    return m_new, acc


BIAS_MID, BIAS_DIAG, BIAS_FIRST = 0, 1, 2


SEL_GROUP = 2


def _sel_win_kernel(lists_ref, nstage_ref, q_ref, kk_ref, vv_ref, sel_ref, oc_ref, gl_ref, slope_ref,
                    bias_ref, o_ref, m_ref, acc_ref, sa_ref, sb_ref, *, nq, stride):
    tps = q_ref.shape[2]
    lax.fori_loop(0, tps, functools.partial(_sel_win_tile, lists_ref, nstage_ref, q_ref, kk_ref, vv_ref, sel_ref,
                                            oc_ref, gl_ref, slope_ref, bias_ref, o_ref, m_ref, acc_ref,
                                            sa_ref, sb_ref, nq, stride, tps), 0)


def _sel_win_tile(lists_ref, nstage_ref, q_ref, kk_ref, vv_ref, sel_ref, oc_ref, gl_ref, slope_ref,
                  bias_ref, o_ref, m_ref, acc_ref, sa_ref, sb_ref, nq, stride, tps, ti, carry):
    bi, gi, n = pl.program_id(0), pl.program_id(1), pl.program_id(2) * tps + ti
    qt = q_ref[0, 0, ti]
    slope = slope_ref[0]
    tile_id = (bi * NSA_G + gi) * nq + n
    base = tile_id * stride
    nstage = nstage_ref[tile_id]
    m_ref[...] = jnp.full(m_ref.shape, NEG_INF, F32)
    acc_ref[...] = jnp.zeros(acc_ref.shape, F32)

    def entry(it, u):
        code = lists_ref[base + jnp.minimum(it, nstage) * SEL_GROUP + u]
        c = jnp.maximum(code, 0)
        return c >> CODE_SHIFT, c & ((1 << CODE_SHIFT) - 1), code >= 0

    def score_stage(it, s_ref):
        for u in range(SEL_GROUP):
            br, p, _ = entry(it, u)
            first = (br == WINDOWED) & (p == n - WINDOW // KT)
            kind = jnp.where(p == n, BIAS_DIAG, jnp.where(first, BIAS_FIRST, BIAS_MID))
            rows = pl.ds(pl.multiple_of(p * KT, KT), KT)
            s_ref[u] = _scores(qt, kk_ref[0, 0, br, rows, :], bias_ref[0, kind])

    def softmax_stage(it, s_ref):
        tiles = []
        for u in range(SEL_GROUP):
            br, p, live = entry(it, u)
            keep = jnp.where(live, 1.0, 0.0)
            everyone = jnp.where(br == WINDOWED, 1.0, 0.0)
            chosen = [keep * jnp.maximum(jnp.concatenate([sel_ref[0, 0, ti, pl.ds(2 * p + i, 1), :]] * NSA_R,
                                                         axis=1), everyone) for i in (0, 1)]
            tiles.append((vv_ref[0, 0, br, p], slope * ((n - p) * KT).astype(F32), chosen))
        sb, _, _ = entry(it, 0)
        m, acc = _flash_step([s_ref[u] for u in range(SEL_GROUP)], tiles, (m_ref[sb], acc_ref[sb]))
        m_ref[sb] = m
        acc_ref[sb] = acc

    def body(k, carry):
        score_stage(2 * k + 1, sb_ref)
        softmax_stage(2 * k, sa_ref)
        score_stage(2 * k + 2, sa_ref)
        softmax_stage(2 * k + 1, sb_ref)
        return carry

    score_stage(0, sa_ref)
    lax.fori_loop(0, (nstage + 1) // 2, body, 0)
    d = NSA_HD
    o_s = acc_ref[SELECTED, :d, :] * (1.0 / acc_ref[SELECTED, d:d + 1, :])
    o_w = acc_ref[WINDOWED, :d, :] * (1.0 / acc_ref[WINDOWED, d:d + 1, :])

    gates = jax.nn.sigmoid(gl_ref[0, 0, ti])
    o = gates[0:1] * oc_ref[0, 0, ti].astype(F32) + gates[1:2] * o_s + gates[2:3] * o_w
    pieces = [jnp.concatenate([o[:, r * QB:(r + 1) * QB], o[:, (r + 1) * QB:(r + 2) * QB]], axis=0).T
              for r in range(0, NSA_R, 2)]
    o_ref[0, pl.ds(pl.multiple_of(ti * QB, QB), QB), :] = jnp.concatenate(pieces, axis=1).astype(o_ref.dtype)
    return carry


def tile_lists(pair_flags):
    nq = pair_flags.shape[-1]
    nwin = WINDOW // KT
    n = jnp.arange(nq, dtype=jnp.int32)[:, None]
    wslots = -(-(nwin + 1) // SEL_GROUP) * SEL_GROUP
    wp = n - jnp.arange(wslots, dtype=jnp.int32)[None, :]
    wcode = jnp.where((wp >= 0) & (wp >= n - nwin), (WINDOWED << CODE_SHIFT) + wp, -1)
    p = jnp.arange(nq, dtype=jnp.int32)[None, :]
    active = ((pair_flags != 0) & (p < n)) | (p == n)
    order = jnp.sort(jnp.where(active, p, nq), axis=-1)
    sslots = -(-nq // SEL_GROUP) * SEL_GROUP
    order = jnp.pad(order, ((0, 0),) * 3 + ((0, sslots - nq),), constant_values=nq)
    scode = jnp.where(order < nq, (SELECTED << CODE_SHIFT) + order, -1)
    lead = scode.shape[:3]
    codes = jnp.concatenate([jnp.broadcast_to(wcode, lead + (wslots,)), scode,
                             jnp.full(lead + (2 * SEL_GROUP,), -1, jnp.int32)], axis=-1).astype(jnp.int32)
    cnt = jnp.sum(active, axis=-1).astype(jnp.int32)
    stages = wslots // SEL_GROUP + (cnt + SEL_GROUP - 1) // SEL_GROUP
    return codes.reshape(-1), stages.reshape(-1), codes.shape[-1]


def sel_win_attention(codes, stages, stride, q_t, kk, vv, sel, o_c, gl, slope_row, bias):
    b, g, nq, d, _ = q_t.shape
    t = kk.shape[3]
    ns = sel.shape[3]
    assert nq < (1 << CODE_SHIFT)
    tps = TILES_PER_STEP
    tile = lambda shape: pl.BlockSpec((1, 1, tps) + shape, lambda i, j, n, c, s: (i, j, n, 0, 0))
    grid_spec = pltpu.PrefetchScalarGridSpec(
        num_scalar_prefetch=2,
        grid=(b, g, nq // tps),
        in_specs=[
            tile((d, HQ)),
            pl.BlockSpec((1, 1, 2, t, d), lambda i, j, n, c, s: (i, j, 0, 0, 0)),
            pl.BlockSpec((1, 1, 2, t // KT, VROWS, KT), lambda i, j, n, c, s: (i, j, 0, 0, 0, 0)),
            tile((ns, QB)), tile((d, HQ)), tile((3, HQ)),
            pl.BlockSpec((1, 1, HQ), lambda i, j, n, c, s: (j, 0, 0)),
            pl.BlockSpec((1, 3, KT, HQ), lambda i, j, n, c, s: (j, 0, 0, 0)),
        ],
        out_specs=pl.BlockSpec((1, tps * QB, NSA_R * d), lambda i, j, n, c, s: (i, n, j)),
        scratch_shapes=[pltpu.VMEM((2, 1, HQ), F32), pltpu.VMEM((2, VROWS, HQ), F32),
                        pltpu.VMEM((SEL_GROUP, KT, HQ), F32), pltpu.VMEM((SEL_GROUP, KT, HQ), F32)],
    )
    return pl.pallas_call(
        functools.partial(_sel_win_kernel, nq=nq, stride=stride),
        grid_spec=grid_spec,
        out_shape=jax.ShapeDtypeStruct((b, t, g * NSA_R * d), BF16),
        compiler_params=_cparams(("parallel", "parallel", "arbitrary")),
    )(codes, stages, q_t, kk, vv, sel, o_c, gl, slope_row, bias)


def _row_tile(m):
    return 512 if m % 512 == 0 else m


def even_layer_heads(h, norm_g, w_in, lb, hgrn_norm, ret_norm):
    b, t, dm = h.shape
    proj = norm_matmul(h.reshape(b * t, dm), norm_g, w_in.astype(BF16), _row_tile(b * t), 1024, BF16)
    o = even_mix(proj.reshape(b, t, EVEN_IN), lb, hgrn_norm, ret_norm, min(t, 512))
    return o.reshape(b * t, -1)


def odd_layer_heads(h, norm_g, w_in, cmp_pos_k, cmp_pos_v, cmp_w1_k, cmp_w2_k, cmp_w1_v, cmp_w2_v):
    b, t, dm = h.shape
    g_, r_, d = NSA_G, NSA_R, NSA_HD
    nq = t // QB
    ns = t // SLC_BLOCK
    nch = t // CMP_STRIDE
    q_t, kc, vc, kk, vv, gl_t = odd_project(h, norm_g, w_in, min(t, 512), (d ** -0.5) * LOG2E)
    chunks = lambda a: a.reshape(b, g_, nch, CMP_STRIDE * d)
    k_cmp, v_cmp = compress(chunks(kc), chunks(vc), cmp_pos_k, cmp_pos_v,
                            cmp_w1_k.astype(BF16), cmp_w1_v.astype(BF16),
                            cmp_w2_k.astype(BF16), cmp_w2_v.astype(BF16))
    v_cmp_t = v_cmp.transpose(0, 1, 3, 2)

    slopes = jnp.exp2(-8.0 * jnp.arange(1, NSA_HEADS + 1, dtype=F32) / NSA_HEADS).reshape(g_, r_)
    slope_row = jnp.repeat(slopes, QB, axis=1).reshape(g_, 1, HQ) * LOG2E
    tq = np.tile(np.arange(QB), r_)[None, :]
    c_rel = np.arange(-(QB // CMP_STRIDE) * (nq - 1), nch)[:, None]
    dist_rel = tq - (c_rel * CMP_STRIDE + CMP_BLOCK - 1)
    cmp_bias = (slope_row * jnp.asarray(dist_rel, F32)[None]
                + jnp.asarray(np.where(dist_rel >= 0, 0.0, -NEG_INF), F32)[None])
    rel = tq - np.arange(KT)[:, None]
    mid = slope_row * jnp.asarray(rel, F32)[None]
    bias = jnp.stack([mid, mid + jnp.asarray(np.where(rel >= 0, 0.0, -NEG_INF), F32),
                      mid + jnp.asarray(np.where(rel < 0, 0.0, -NEG_INF), F32)], axis=1)
    c0 = np.arange(nch) * CMP_STRIDE
    s0 = np.arange(ns) * SLC_BLOCK
    ov = np.maximum(np.minimum(c0[:, None] + CMP_BLOCK, s0[None, :] + SLC_BLOCK)
                    - np.maximum(c0[:, None], s0[None, :]), 0) / CMP_BLOCK
    ov[nch - (CMP_BLOCK // CMP_STRIDE - 1):] = 0.0
    ovt = jnp.asarray(ov.T, BF16)

    o_c, sel, flags = cmp_attention(q_t, k_cmp, v_cmp_t, cmp_bias, ovt)
    codes, stages, stride = tile_lists(flags.reshape(b, g_, nq, ns // 2, 2).max(axis=-1))
    o = sel_win_attention(codes, stages, stride, q_t, kk, vv, sel, o_c, gl_t, slope_row, bias)
    return o.reshape(b * t, NSA_HEADS * d)


def finish_layer(h, heads, w_out, norm_g, w_gate_up, w_down, final_g):
    b, t, dm = h.shape
    m = b * t
    out = mixer_out_ffn(h.reshape(m, dm), heads, w_out.astype(BF16), norm_g, w_gate_up.astype(BF16),
                        w_down.astype(BF16), final_g, 1024 if m % 1024 == 0 else m, 256)
    return out.reshape(b, t, dm)


def kernel(x, mix_norm, ffn_norm, final_norm, even_w_in, hgrn_lower_bounds, hgrn_out_norm, ret_out_norm,
           even_w_out, odd_w_in, cmp_pos_k, cmp_pos_v, cmp_w1_k, cmp_w2_k, cmp_w1_v, cmp_w2_v, odd_w_out,
           ffn_w_gate_up, ffn_w_down):
    depth = mix_norm.shape[0]
    lb_all = jnp.cumsum(jax.nn.softmax(hgrn_lower_bounds.astype(F32), axis=0), axis=0)
    h = x
    for layer in range(depth):
        if layer % 2 == 0:
            e = layer // 2
            heads = even_layer_heads(h, mix_norm[layer], even_w_in[e], lb_all[e], hgrn_out_norm[e], ret_out_norm[e])
            w_out = even_w_out[e]
        else:
            o = layer // 2
            heads = odd_layer_heads(h, mix_norm[layer], odd_w_in[o], cmp_pos_k[o], cmp_pos_v[o], cmp_w1_k[o],
                                    cmp_w2_k[o], cmp_w1_v[o], cmp_w2_v[o])
            w_out = odd_w_out[o]
        h = finish_layer(h, heads, w_out, ffn_norm[layer], ffn_w_gate_up[layer], ffn_w_down[layer],
                         final_norm if layer == depth - 1 else None)
    return h
```
